```python
import jax, jax.numpy as jnp
from jax import lax
import numpy as np

D_MODEL = 4096
BATCH = 2
SEQ = 8192
DEPTH = 2

CHUNK = 64
HEAD_DIM = 128
D_MIX = D_MODEL
D_RET = D_MIX // 2
D_ATT = D_MIX - D_RET
N_RET_HEADS = D_RET // HEAD_DIM
N_ATT_HEADS = D_ATT // HEAD_DIM
D_IN = 4 * D_RET + 3 * D_ATT
LEFT_CHUNKS = 8
LEFT = LEFT_CHUNKS * CHUNK
BAND = (LEFT_CHUNKS + 1) * CHUNK
REL_CLIP = 128
N_REL = REL_CLIP + CHUNK
D_FF = 256 * ((8 * D_MODEL // 3 + 255) // 256)
CONV_WIDTH = 3
ROPE_BASE = 10000.0
EPS = 1e-6

kernel_name = "hybrid_retention_chunkattn_convffn"


def rms_norm(x, gain):
    xf = x.astype(jnp.float32)
    y = xf * lax.rsqrt(jnp.mean(xf * xf, axis=-1, keepdims=True) + EPS)
    return (y * gain.astype(jnp.float32)).astype(x.dtype)


def rotary(x):
    t = x.shape[1]
    half = HEAD_DIM // 2
    inv = 1.0 / (ROPE_BASE ** jnp.linspace(0.0, 1.0, half, dtype=jnp.float32))
    ang = jnp.arange(t, dtype=jnp.float32)[:, None] * inv[None, :]
    cos = jnp.cos(ang)[None, :, None, :]
    sin = jnp.sin(ang)[None, :, None, :]
    xf = x.astype(jnp.float32)
    x1, x2 = xf[..., :half], xf[..., half:]
    return jnp.concatenate([x1 * cos - x2 * sin, x1 * sin + x2 * cos], axis=-1)


def retention(q, k, v):
    b, t, h, dh = q.shape
    nc = t // CHUNK
    log_g = jnp.log(1.0 - 2.0 ** (-5.0 - jnp.arange(h, dtype=jnp.float32)))
    pos = jnp.arange(CHUNK, dtype=jnp.float32)
    diff = pos[:, None] - pos[None, :]
    intra = jnp.where(diff[None] >= 0,
                      jnp.exp(jnp.maximum(diff, 0.0)[None] * log_g[:, None, None]), 0.0)
    q_decay = jnp.exp((pos[:, None] + 1.0) * log_g[None, :])
    k_decay = jnp.exp((CHUNK - 1.0 - pos[:, None]) * log_g[None, :])
    chunk_decay = jnp.exp(CHUNK * log_g)

    def to_chunks(a):
        return a.astype(jnp.float32).reshape(b, nc, CHUNK, h, dh).transpose(1, 0, 2, 3, 4)

    qc = to_chunks(q)
    kc = to_chunks(k) * (dh ** -0.5)
    vc = to_chunks(v)

    def step(state, xs):
        qi, ki, vi = xs
        s = jnp.einsum('bihd,bjhd->bhij', qi, ki) * intra[None]
        o = jnp.einsum('bhij,bjhe->bihe', s, vi)
        o = o + jnp.einsum('bihd,bhde->bihe', qi, state) * q_decay[None, :, :, None]
        state = (state * chunk_decay[None, :, None, None]
                 + jnp.einsum('bjhd,bjhe->bhde', ki * k_decay[None, :, :, None], vi))
        return state, o

    s0 = jnp.zeros((b, h, dh, dh), jnp.float32)
    _, o = lax.scan(step, s0, (qc, kc, vc))
    return o.transpose(1, 0, 2, 3, 4).reshape(b, t, h, dh)


def chunk_attention(q, k, v, rel_table):
    b, t, h, dh = q.shape
    nc = t // CHUNK
    qi = jnp.arange(CHUNK)[:, None]
    kj = jnp.arange(BAND)
    rel = jnp.clip(qi + LEFT - kj[None, :], -(CHUNK - 1), REL_CLIP) + (CHUNK - 1)
    bias = rel_table.astype(jnp.float32)[:, rel]
    pad = ((0, 0), (LEFT, 0), (0, 0), (0, 0))
    kp = jnp.pad(k, pad)
    vp = jnp.pad(v, pad)
    qc = q.reshape(b, nc, CHUNK, h, dh).transpose(1, 0, 2, 3, 4)
    scale = dh ** -0.5

    def one_chunk(args):
        c, qb = args
        start = c * CHUNK
        kb = lax.dynamic_slice_in_dim(kp, start, BAND, axis=1)
        vb = lax.dynamic_slice_in_dim(vp, start, BAND, axis=1)
        s = jnp.einsum('bihd,bjhd->bhij', qb, kb).astype(jnp.float32) * scale + bias[None]
        valid = kj >= LEFT - start
        s = jnp.where(valid[None, None, None, :], s, -jnp.inf)
        p = jax.nn.softmax(s, axis=-1).astype(vb.dtype)
        return jnp.einsum('bhij,bjhd->bihd', p, vb)

    o = lax.map(one_chunk, (jnp.arange(nc), qc))
    return o.transpose(1, 0, 2, 3, 4).reshape(b, t, h, dh)


def hybrid_mixer(h, ln, w_in, w_out, rel_table):
    b, t, _ = h.shape
    xn = rms_norm(h, ln)
    proj = xn @ w_in
    splits = [D_RET, 2 * D_RET, 3 * D_RET, 4 * D_RET, 4 * D_RET + D_ATT, 4 * D_RET + 2 * D_ATT]
    rq, rk, rv, rg, aq, ak, av = jnp.split(proj, splits, axis=-1)
    rq = rotary(rq.reshape(b, t, N_RET_HEADS, HEAD_DIM))
    rk = rotary(rk.reshape(b, t, N_RET_HEADS, HEAD_DIM))
    rv = rv.reshape(b, t, N_RET_HEADS, HEAD_DIM)
    ro = retention(rq, rk, rv)
    ro = ro * lax.rsqrt(jnp.mean(ro * ro, axis=-1, keepdims=True) + EPS)
    ro = jax.nn.silu(rg.astype(jnp.float32)).reshape(b, t, N_RET_HEADS, HEAD_DIM) * ro
    ro = ro.reshape(b, t, D_RET).astype(h.dtype)
    ao = chunk_attention(aq.reshape(b, t, N_ATT_HEADS, HEAD_DIM),
                         ak.reshape(b, t, N_ATT_HEADS, HEAD_DIM),
                         av.reshape(b, t, N_ATT_HEADS, HEAD_DIM), rel_table)
    ao = ao.reshape(b, t, D_ATT).astype(h.dtype)
    return jnp.concatenate([ro, ao], axis=-1) @ w_out


def conv_ffn(h, ln, w_up, conv_w, conv_b, w_down):
    xn = rms_norm(h, ln)
    u = xn @ w_up
    u = lax.conv_general_dilated(u, conv_w[:, None, :], window_strides=(1,),
                                 padding=[(CONV_WIDTH - 1, 0)],
                                 dimension_numbers=('NWC', 'WIO', 'NWC'),
                                 feature_group_count=2 * D_FF) + conv_b
    g, val = jnp.split(u, 2, axis=-1)
    return (jax.nn.silu(g) * val) @ w_down


def setup_inputs(seed: int = 0) -> dict:
    key = jax.random.key(seed)
    ks = jax.random.split(key, 12)
    res_scale = (2.0 * DEPTH) ** -0.5
    f32 = jnp.float32
    x = jax.random.normal(ks[0], (BATCH, SEQ, D_MODEL), f32)
    ln_mix = 1.0 + 0.02 * jax.random.normal(ks[1], (DEPTH, D_MODEL), f32)
    w_in = jax.random.normal(ks[2], (DEPTH, D_MODEL, D_IN), f32) * D_MODEL ** -0.5
    rel_bias = 0.2 * jax.random.normal(ks[3], (DEPTH, N_ATT_HEADS, N_REL), f32)
    w_out = jax.random.normal(ks[4], (DEPTH, D_MIX, D_MODEL), f32) * (D_MIX ** -0.5 * res_scale)
    ln_ffn = 1.0 + 0.02 * jax.random.normal(ks[5], (DEPTH, D_MODEL), f32)
    w_up = jax.random.normal(ks[6], (DEPTH, D_MODEL, 2 * D_FF), f32) * D_MODEL ** -0.5
    conv_w = jax.random.normal(ks[7], (DEPTH, CONV_WIDTH, 2 * D_FF), f32) * CONV_WIDTH ** -0.5
    conv_b = 0.02 * jax.random.normal(ks[8], (DEPTH, 2 * D_FF), f32)
    w_down = jax.random.normal(ks[9], (DEPTH, D_FF, D_MODEL), f32) * (D_FF ** -0.5 * res_scale)
    ln_final = 1.0 + 0.02 * jax.random.normal(ks[10], (D_MODEL,), f32)
    return {"x": x, "ln_mix": ln_mix, "w_in": w_in, "rel_bias": rel_bias, "w_out": w_out,
            "ln_ffn": ln_ffn, "w_up": w_up, "conv_w": conv_w, "conv_b": conv_b,
            "w_down": w_down, "ln_final": ln_final}


def reference(x, ln_mix, w_in, rel_bias, w_out, ln_ffn, w_up, conv_w, conv_b, w_down, ln_final):
    h = x
    for layer in range(DEPTH):
        h = h + hybrid_mixer(h, ln_mix[layer], w_in[layer], w_out[layer], rel_bias[layer])
        h = h + conv_ffn(h, ln_ffn[layer], w_up[layer], conv_w[layer], conv_b[layer], w_down[layer])
    return rms_norm(h, ln_final)
```

```python
import functools

import numpy as np
import jax
import jax.numpy as jnp
from jax import lax
from jax.experimental import pallas as pl
from jax.experimental.pallas import tpu as pltpu

CHUNK = 64
HEAD_DIM = 128
LEFT_CHUNKS = 8
LEFT = LEFT_CHUNKS * CHUNK
REL_CLIP = 128
N_REL = REL_CLIP + CHUNK
CONV_WIDTH = 3
ROPE_BASE = 10000.0
EPS = 1e-6

LANES = 128
SUBLANES = 8
VMEM_LIMIT = 56 * 1024 * 1024
FF_ALIGN = 1024
MASKED = -1e30

F32 = jnp.float32
BF16 = jnp.bfloat16


def _params(*sem):
    return pltpu.CompilerParams(dimension_semantics=sem, vmem_limit_bytes=VMEM_LIMIT)


def _tile(n, want):
    t = min(n, want)
    while n % t:
        t //= 2
    return t


def _rmsnorm_body(x_ref, g_ref, o_ref):
    x = x_ref[...]
    ms = jnp.mean(x * x, axis=-1, keepdims=True)
    o_ref[...] = (x * lax.rsqrt(ms + EPS) * g_ref[...]).astype(o_ref.dtype)


def rmsnorm(x, gain, out_dtype):
    m, d = x.shape
    tm = _tile(m, 256)
    return pl.pallas_call(
        _rmsnorm_body,
        grid=(m // tm,),
        in_specs=[pl.BlockSpec((tm, d), lambda i: (i, 0)),
                  pl.BlockSpec((1, d), lambda i: (0, 0))],
        out_specs=pl.BlockSpec((tm, d), lambda i: (i, 0)),
        out_shape=jax.ShapeDtypeStruct((m, d), out_dtype),
        compiler_params=_params("parallel"),
        name="rmsnorm",
    )(x, gain.reshape(1, d))


def _mm_body(a_ref, b_ref, o_ref):
    o_ref[...] = jnp.dot(a_ref[...], b_ref[...],
                         preferred_element_type=F32).astype(o_ref.dtype)


def matmul(a, b, out_dtype, name):
    m, k = a.shape
    n = b.shape[1]
    tm, tn = _tile(m, 1024), _tile(n, 1024)
    return pl.pallas_call(
        _mm_body,
        grid=(m // tm, n // tn),
        in_specs=[pl.BlockSpec((tm, k), lambda i, j: (i, 0)),
                  pl.BlockSpec((k, tn), lambda i, j: (0, j))],
        out_specs=pl.BlockSpec((tm, tn), lambda i, j: (i, j)),
        out_shape=jax.ShapeDtypeStruct((m, n), out_dtype),
        compiler_params=_params("parallel", "arbitrary"),
        name=name,
    )(a, b)


def _mm2_res_body(a1_ref, a2_ref, b1_ref, b2_ref, r_ref, o_ref):
    acc = jnp.dot(a1_ref[...], b1_ref[...], preferred_element_type=F32)
    acc += jnp.dot(a2_ref[...], b2_ref[...], preferred_element_type=F32)
    o_ref[...] = r_ref[...] + acc


def out_proj_residual(a1, a2, w, res):
    m, k1 = a1.shape
    k2 = a2.shape[1]
    n = w.shape[1]
    assert k1 == k2 and w.shape[0] == k1 + k2
    tm, tn = _tile(m, 512), _tile(n, 1024)
    return pl.pallas_call(
        _mm2_res_body,
        grid=(m // tm, n // tn),
        in_specs=[pl.BlockSpec((tm, k1), lambda i, j: (i, 0)),
                  pl.BlockSpec((tm, k2), lambda i, j: (i, 0)),
                  pl.BlockSpec((k1, tn), lambda i, j: (0, j)),
                  pl.BlockSpec((k2, tn), lambda i, j: (1, j)),
                  pl.BlockSpec((tm, tn), lambda i, j: (i, j))],
        out_specs=pl.BlockSpec((tm, tn), lambda i, j: (i, j)),
        out_shape=jax.ShapeDtypeStruct((m, n), F32),
        compiler_params=_params("parallel", "arbitrary"),
        name="out_proj",
    )(a1, a2, w, w, res)


def _mm_res_k_body(a_ref, b_ref, r_ref, o_ref, acc_ref):
    k = pl.program_id(2)

    @pl.when(k == 0)
    def _():
        acc_ref[...] = jnp.zeros_like(acc_ref)

    acc_ref[...] += jnp.dot(a_ref[...], b_ref[...], preferred_element_type=F32)

    @pl.when(k == pl.num_programs(2) - 1)
    def _():
        o_ref[...] = r_ref[...] + acc_ref[...]


def down_proj_residual(a, w, res):
    m, k = a.shape
    n = w.shape[1]
    tm, tn = _tile(m, 1024), _tile(n, 1024)
    tk = k // 4 if (k // 4) % LANES == 0 else k
    return pl.pallas_call(
        _mm_res_k_body,
        grid=(m // tm, n // tn, k // tk),
        in_specs=[pl.BlockSpec((tm, tk), lambda i, j, kk: (i, kk)),
                  pl.BlockSpec((tk, tn), lambda i, j, kk: (kk, j)),
                  pl.BlockSpec((tm, tn), lambda i, j, kk: (i, j))],
        out_specs=pl.BlockSpec((tm, tn), lambda i, j, kk: (i, j)),
        out_shape=jax.ShapeDtypeStruct((m, n), F32),
        scratch_shapes=[pltpu.VMEM((tm, tn), F32)],
        compiler_params=_params("parallel", "arbitrary", "arbitrary"),
        name="down_proj",
    )(a, w, res)


def _rope_body(inv_ref, cos_ref, sin_ref):
    tt = cos_ref.shape[0]
    row = lax.broadcasted_iota(jnp.int32, (tt, LANES), 0) + pl.program_id(0) * tt
    lane = lax.broadcasted_iota(jnp.int32, (tt, LANES), 1)
    ang = row.astype(F32) * inv_ref[...]
    cos_ref[...] = jnp.cos(ang)
    sin_ref[...] = jnp.where(lane < HEAD_DIM // 2, -1.0, 1.0) * jnp.sin(ang)


def rope_tables(t):
    half = HEAD_DIM // 2
    inv = 1.0 / (ROPE_BASE ** jnp.linspace(0.0, 1.0, half, dtype=F32))
    inv2 = jnp.concatenate([inv, inv]).reshape(1, HEAD_DIM)
    tt = _tile(t, 512)
    return pl.pallas_call(
        _rope_body,
        grid=(t // tt,),
        in_specs=[pl.BlockSpec((1, HEAD_DIM), lambda i: (0, 0))],
        out_specs=[pl.BlockSpec((tt, HEAD_DIM), lambda i: (i, 0)),
                   pl.BlockSpec((tt, HEAD_DIM), lambda i: (i, 0))],
        out_shape=[jax.ShapeDtypeStruct((t, HEAD_DIM), F32)] * 2,
        compiler_params=_params("parallel"),
        name="rope_tables",
    )(inv2)


def _silu(x):
    return x * jax.nn.sigmoid(x)


def _retention_body(logg_ref, q_ref, k_ref, v_ref, g_ref, cos_ref, sin_ref, o_ref, state_ref):
    head = pl.program_id(1)
    step = pl.program_id(2)
    c = q_ref.shape[0]

    @pl.when(step == 0)
    def _():
        state_ref[...] = jnp.zeros_like(state_ref)

    lg = logg_ref[head]
    cos = cos_ref[...]
    sin = sin_ref[...]

    def rot(x):
        return x * cos + pltpu.roll(x, HEAD_DIM // 2, axis=1) * sin

    q = rot(q_ref[...].astype(F32))
    k = rot(k_ref[...].astype(F32)) * (HEAD_DIM ** -0.5)
    v = v_ref[...]

    row = lax.broadcasted_iota(jnp.int32, (c, c), 0)
    col = lax.broadcasted_iota(jnp.int32, (c, c), 1)
    diff = (row - col).astype(F32)
    intra = jnp.where(diff >= 0, jnp.exp(jnp.maximum(diff, 0.0) * lg), 0.0)
    pos = lax.broadcasted_iota(jnp.int32, (c, 1), 0).astype(F32)
    q_decay = jnp.exp((pos + 1.0) * lg)
    k_decay = jnp.exp((c - 1.0 - pos) * lg)
    block_decay = jnp.exp(jnp.full((1, HEAD_DIM), float(c), F32) * lg)

    qb = q.astype(BF16)
    s = lax.dot_general(qb, k.astype(BF16), (((1,), (1,)), ((), ())),
                        preferred_element_type=F32) * intra
    state = state_ref[...]
    o = jnp.dot(s.astype(BF16), v, preferred_element_type=F32)
    o += jnp.dot(qb, state.astype(BF16), preferred_element_type=F32) * q_decay
    kv = lax.dot_general((k * k_decay).astype(BF16), v, (((0,), (0,)), ((), ())),
                         preferred_element_type=F32)
    state_ref[...] = state * block_decay + kv

    o = o * lax.rsqrt(jnp.mean(o * o, axis=-1, keepdims=True) + EPS)
    o_ref[...] = (_silu(g_ref[...].astype(F32)) * o).astype(o_ref.dtype)


def retention(proj, cos_t, sin_t, log_g, batch, t, n_heads, block):
    m = batch * t
    nt = t // block
    hd = HEAD_DIM

    def col(which):
        return pl.BlockSpec((block, hd), lambda b, h, i, lg: (b * nt + i, which * n_heads + h))

    tab = pl.BlockSpec((block, hd), lambda b, h, i, lg: (i, 0))
    return pl.pallas_call(
        _retention_body,
        grid_spec=pltpu.PrefetchScalarGridSpec(
            num_scalar_prefetch=1,
            grid=(batch, n_heads, nt),
            in_specs=[col(0), col(1), col(2), col(3), tab, tab],
            out_specs=pl.BlockSpec((block, hd), lambda b, h, i, lg: (b * nt + i, h)),
            scratch_shapes=[pltpu.VMEM((hd, hd), F32)],
        ),
        out_shape=jax.ShapeDtypeStruct((m, n_heads * hd), BF16),
        compiler_params=_params("parallel", "parallel", "arbitrary"),
        name="retention",
    )(log_g, proj, proj, proj, proj, cos_t, sin_t)


def _att_bias_body(tab_ref, o_ref):
    qb, width = o_ref.shape[1], o_ref.shape[2]
    n = pl.cdiv(width + qb - 1, LANES) * LANES
    tab = jnp.broadcast_to(tab_ref[0], (SUBLANES, N_REL))
    u = lax.broadcasted_iota(jnp.int32, (N_REL, n), 1)
    kk = lax.broadcasted_iota(jnp.int32, (N_REL, n), 0)
    rel = jnp.clip(LEFT + qb - 1 - u, -(CHUNK - 1), REL_CLIP) + (CHUNK - 1)
    onehot = jnp.where(rel == kk, 1.0, 0.0).astype(BF16)
    e = jnp.zeros((SUBLANES, n), F32)
    rest = tab
    for _ in range(3):
        piece = rest.astype(BF16)
        e += jnp.dot(piece, onehot, preferred_element_type=F32)
        rest = rest - piece.astype(F32)
    eb = jnp.broadcast_to(e[0:1], (qb, n))
    bias = pltpu.roll(eb, n - (qb - 1), axis=1, stride=1, stride_axis=0)[:, :width]
    r = lax.broadcasted_iota(jnp.int32, (qb, width), 0) // CHUNK
    mc = lax.broadcasted_iota(jnp.int32, (qb, width), 1) // CHUNK
    in_band = (mc >= r) & (mc <= r + LEFT_CHUNKS)
    o_ref[0] = jnp.where(in_band, bias, MASKED)


def attention_bias(rel_table, qb):
    h = rel_table.shape[0]
    width = LEFT + qb
    return pl.pallas_call(
        _att_bias_body,
        grid=(h,),
        in_specs=[pl.BlockSpec((1, 1, N_REL), lambda i: (i, 0, 0))],
        out_specs=pl.BlockSpec((1, qb, width), lambda i: (i, 0, 0)),
        out_shape=jax.ShapeDtypeStruct((h, qb, width), F32),
        compiler_params=_params("parallel"),
        name="attention_bias",
    )(rel_table.reshape(h, 1, N_REL))


def _attention_body(nkb, q_ref, *refs):
    k_refs, v_refs = refs[:nkb], refs[nkb:2 * nkb]
    bias_ref, o_ref = refs[2 * nkb], refs[2 * nkb + 1]
    step = pl.program_id(2)
    qb = q_ref.shape[0]
    q = q_ref[...]
    scale = HEAD_DIM ** -0.5
    logits = []
    for j in range(nkb):
        s = lax.dot_general(q, k_refs[j][...], (((1,), (1,)), ((), ())),
                            preferred_element_type=F32)
        s = s * scale + bias_ref[0, :, j * qb:(j + 1) * qb]
        if j < nkb - 1:
            s = jnp.where(step - (nkb - 1 - j) >= 0, s, MASKED)
        logits.append(s)
    mx = functools.reduce(jnp.maximum, [jnp.max(s, axis=-1, keepdims=True) for s in logits])
    acc = jnp.zeros((qb, HEAD_DIM), F32)
    den = jnp.zeros((qb, 1), F32)
    for j in range(nkb):
        e = jnp.exp(logits[j] - mx)
        den += jnp.sum(e, axis=-1, keepdims=True)
        acc += jnp.dot(e.astype(BF16), v_refs[j][...], preferred_element_type=F32)
    o_ref[...] = (acc / den).astype(o_ref.dtype)


def band_attention(proj, bias, batch, t, n_heads, col0):
    m = batch * t
    qb = bias.shape[1]
    nkb = LEFT // qb + 1
    nq = t // qb
    hd = HEAD_DIM

    def kv_spec(which, j):
        back = nkb - 1 - j
        return pl.BlockSpec(
            (qb, hd),
            lambda h, b, i: (b * nq + jnp.maximum(i - back, 0), col0 + which * n_heads + h))

    in_specs = [pl.BlockSpec((qb, hd), lambda h, b, i: (b * nq + i, col0 + h))]
    in_specs += [kv_spec(1, j) for j in range(nkb)]
    in_specs += [kv_spec(2, j) for j in range(nkb)]
    in_specs += [pl.BlockSpec((1, qb, LEFT + qb), lambda h, b, i: (h, 0, 0))]
    return pl.pallas_call(
        functools.partial(_attention_body, nkb),
        grid=(n_heads, batch, nq),
        in_specs=in_specs,
        out_specs=pl.BlockSpec((qb, hd), lambda h, b, i: (b * nq + i, h)),
        out_shape=jax.ShapeDtypeStruct((m, n_heads * hd), BF16),
        compiler_params=_params("parallel", "parallel", "arbitrary"),
        name="band_attention",
    )(*([proj] * (1 + 2 * nkb)), bias)


def _conv_act_body(tiles_per_seq, ug_ref, uv_ref, pg_ref, pv_ref, wg_ref, wv_ref,
                   bg_ref, bv_ref, o_ref, sg_ref, sv_ref):
    tt = ug_ref.shape[0]
    seq_start = (pl.program_id(1) % tiles_per_seq) == 0

    def conv(u_ref, p_ref, w_ref, b_ref, s_ref):
        s_ref[SUBLANES:, :] = u_ref[...].astype(F32)
        s_ref[0:SUBLANES, :] = jnp.where(seq_start, 0.0, p_ref[...].astype(F32))
        w = w_ref[...]
        out = b_ref[...] + w[CONV_WIDTH - 1:CONV_WIDTH] * s_ref[SUBLANES:, :]
        for d in range(1, CONV_WIDTH):
            tap = CONV_WIDTH - 1 - d
            out += w[tap:tap + 1] * s_ref[SUBLANES - d:SUBLANES - d + tt, :]
        return out

    g = conv(ug_ref, pg_ref, wg_ref, bg_ref, sg_ref)
    val = conv(uv_ref, pv_ref, wv_ref, bv_ref, sv_ref)
    o_ref[...] = (_silu(g) * val).astype(o_ref.dtype)


def conv_gate(u, conv_w, conv_b, t):
    m, f2 = u.shape
    f = f2 // 2
    tt, tn = _tile(t, 512), _tile(f, 512)
    nj = f // tn
    sub = tt // SUBLANES

    def cur(half):
        return pl.BlockSpec((tt, tn), lambda j, i: (i, half * nj + j))

    def prev(half):
        return pl.BlockSpec((SUBLANES, tn),
                            lambda j, i: (jnp.maximum(i * sub - 1, 0), half * nj + j))

    def wspec(half):
        return pl.BlockSpec((CONV_WIDTH, tn), lambda j, i: (0, half * nj + j))

    def bspec(half):
        return pl.BlockSpec((1, tn), lambda j, i: (0, half * nj + j))

    return pl.pallas_call(
        functools.partial(_conv_act_body, t // tt),
        grid=(nj, m // tt),
        in_specs=[cur(0), cur(1), prev(0), prev(1), wspec(0), wspec(1), bspec(0), bspec(1)],
        out_specs=pl.BlockSpec((tt, tn), lambda j, i: (i, j)),
        out_shape=jax.ShapeDtypeStruct((m, f), BF16),
        scratch_shapes=[pltpu.VMEM((SUBLANES + tt, tn), F32)] * 2,
        compiler_params=_params("parallel", "parallel"),
        name="conv_gate",
    )(u, u, u, u, conv_w, conv_w, conv_b.reshape(1, f2), conv_b.reshape(1, f2))


def _pad_ff_cols(w, f, fp):
    lead = w.shape[:-1]
    w = w.reshape(*lead, 2, f)
    w = jnp.pad(w, [(0, 0)] * len(lead) + [(0, 0), (0, fp - f)])
    return w.reshape(*lead, 2 * fp)


def kernel(x, ln_mix, w_in, rel_bias, w_out, ln_ffn, w_up, conv_w, conv_b, w_down, ln_final):
    batch, t, d = x.shape
    depth = w_in.shape[0]
    m = batch * t
    n_ret = (d // 2) // HEAD_DIM
    n_att = rel_bias.shape[1]
    f = w_down.shape[1]
    fp = -(-f // FF_ALIGN) * FF_ALIGN

    cos_t, sin_t = rope_tables(t)
    log_g = jnp.log(1.0 - 2.0 ** (-5.0 - jnp.arange(n_ret, dtype=F32)))
    ret_block = _tile(t, 256)
    att_block = _tile(LEFT, 256)

    h = x.reshape(m, d)
    for layer in range(depth):
        xn = rmsnorm(h, ln_mix[layer], BF16)
        proj = matmul(xn, w_in[layer].astype(BF16), BF16, "in_proj")
        ro = retention(proj, cos_t, sin_t, log_g, batch, t, n_ret, ret_block)
        bias = attention_bias(rel_bias[layer], att_block)
        ao = band_attention(proj, bias, batch, t, n_att, 4 * n_ret)
        h = out_proj_residual(ro, ao, w_out[layer].astype(BF16), h)

        xn = rmsnorm(h, ln_ffn[layer], BF16)
        w_up_p = _pad_ff_cols(w_up[layer], f, fp).astype(BF16)
        u = matmul(xn, w_up_p, BF16, "up_proj")
        act = conv_gate(u, _pad_ff_cols(conv_w[layer], f, fp), _pad_ff_cols(conv_b[layer], f, fp), t)
        w_down_p = jnp.pad(w_down[layer], ((0, fp - f), (0, 0))).astype(BF16)
        h = down_proj_residual(act, w_down_p, h)
    return rmsnorm(h, ln_final, F32).reshape(batch, t, d)
```

```python
import functools

import numpy as np
import jax
import jax.numpy as jnp
from jax import lax
from jax.experimental import pallas as pl
from jax.experimental.pallas import tpu as pltpu

CHUNK = 64
HEAD_DIM = 128
LEFT_CHUNKS = 8
LEFT = LEFT_CHUNKS * CHUNK
REL_CLIP = 128
N_REL = REL_CLIP + CHUNK
CONV_WIDTH = 3
ROPE_BASE = 10000.0
EPS = 1e-6

LANES = 128
SUBLANES = 8
VMEM_LIMIT = 56 * 1024 * 1024
FF_ALIGN = 1024
HEAD_GROUP = 4
MASKED = -1e30

F32 = jnp.float32
BF16 = jnp.bfloat16


def _params(*sem):
    return pltpu.CompilerParams(dimension_semantics=sem, vmem_limit_bytes=VMEM_LIMIT)


def _tile(n, want):
    t = min(n, want)
    while n % t:
        t //= 2
    return t


def _rmsnorm_body(x_ref, g_ref, o_ref):
    x = x_ref[...]
    ms = jnp.mean(x * x, axis=-1, keepdims=True)
    o_ref[...] = (x * lax.rsqrt(ms + EPS) * g_ref[...]).astype(o_ref.dtype)


def rmsnorm(x, gain, out_dtype):
    m, d = x.shape
    tm = _tile(m, 256)
    return pl.pallas_call(
        _rmsnorm_body,
        grid=(m // tm,),
        in_specs=[pl.BlockSpec((tm, d), lambda i: (i, 0)),
                  pl.BlockSpec((1, d), lambda i: (0, 0))],
        out_specs=pl.BlockSpec((tm, d), lambda i: (i, 0)),
        out_shape=jax.ShapeDtypeStruct((m, d), out_dtype),
        compiler_params=_params("parallel"),
        name="rmsnorm",
    )(x, gain.reshape(1, d))


def _mm_body(a_ref, b_ref, o_ref):
    o_ref[...] = jnp.dot(a_ref[...], b_ref[...],
                         preferred_element_type=F32).astype(o_ref.dtype)


def matmul(a, b, out_dtype, name):
    m, k = a.shape
    n = b.shape[1]
    tm, tn = _tile(m, 1024), _tile(n, 1024)
    return pl.pallas_call(
        _mm_body,
        grid=(m // tm, n // tn),
        in_specs=[pl.BlockSpec((tm, k), lambda i, j: (i, 0)),
                  pl.BlockSpec((k, tn), lambda i, j: (0, j))],
        out_specs=pl.BlockSpec((tm, tn), lambda i, j: (i, j)),
        out_shape=jax.ShapeDtypeStruct((m, n), out_dtype),
        compiler_params=_params("parallel", "arbitrary"),
        name=name,
    )(a, b)


def _mm2_res_body(a1_ref, a2_ref, b1_ref, b2_ref, r_ref, o_ref):
    acc = jnp.dot(a1_ref[...], b1_ref[...], preferred_element_type=F32)
    acc += jnp.dot(a2_ref[...], b2_ref[...], preferred_element_type=F32)
    o_ref[...] = r_ref[...] + acc


def out_proj_residual(a1, a2, w, res):
    m, k1 = a1.shape
    k2 = a2.shape[1]
    n = w.shape[1]
    assert k1 == k2 and w.shape[0] == k1 + k2
    tm, tn = _tile(m, 512), _tile(n, 1024)
    return pl.pallas_call(
        _mm2_res_body,
        grid=(m // tm, n // tn),
        in_specs=[pl.BlockSpec((tm, k1), lambda i, j: (i, 0)),
                  pl.BlockSpec((tm, k2), lambda i, j: (i, 0)),
                  pl.BlockSpec((k1, tn), lambda i, j: (0, j)),
                  pl.BlockSpec((k2, tn), lambda i, j: (1, j)),
                  pl.BlockSpec((tm, tn), lambda i, j: (i, j))],
        out_specs=pl.BlockSpec((tm, tn), lambda i, j: (i, j)),
        out_shape=jax.ShapeDtypeStruct((m, n), F32),
        compiler_params=_params("parallel", "arbitrary"),
        name="out_proj",
    )(a1, a2, w, w, res)


def _mm_res_k_body(a_ref, b_ref, r_ref, o_ref, acc_ref):
    k = pl.program_id(2)

    @pl.when(k == 0)
    def _():
        acc_ref[...] = jnp.zeros_like(acc_ref)

    acc_ref[...] += jnp.dot(a_ref[...], b_ref[...], preferred_element_type=F32)

    @pl.when(k == pl.num_programs(2) - 1)
    def _():
        o_ref[...] = r_ref[...] + acc_ref[...]


def down_proj_residual(a, w, res):
    m, k = a.shape
    n = w.shape[1]
    tm, tn = _tile(m, 1024), _tile(n, 1024)
    tk = k // 4 if (k // 4) % LANES == 0 else k
    return pl.pallas_call(
        _mm_res_k_body,
        grid=(m // tm, n // tn, k // tk),
        in_specs=[pl.BlockSpec((tm, tk), lambda i, j, kk: (i, kk)),
                  pl.BlockSpec((tk, tn), lambda i, j, kk: (kk, j)),
                  pl.BlockSpec((tm, tn), lambda i, j, kk: (i, j))],
        out_specs=pl.BlockSpec((tm, tn), lambda i, j, kk: (i, j)),
        out_shape=jax.ShapeDtypeStruct((m, n), F32),
        scratch_shapes=[pltpu.VMEM((tm, tn), F32)],
        compiler_params=_params("parallel", "arbitrary", "arbitrary"),
        name="down_proj",
    )(a, w, res)


def _cast_body(x_ref, o_ref):
    o_ref[...] = x_ref[...].astype(o_ref.dtype)


def cast_weight(w, layer):
    _, k, n = w.shape
    tk, tn = _tile(k, 1024), _tile(n, 2048)
    return pl.pallas_call(
        _cast_body,
        grid=(k // tk, n // tn),
        in_specs=[pl.BlockSpec((None, tk, tn), lambda i, j: (layer, i, j))],
        out_specs=pl.BlockSpec((tk, tn), lambda i, j: (i, j)),
        out_shape=jax.ShapeDtypeStruct((k, n), BF16),
        compiler_params=_params("parallel", "parallel"),
        name="cast_weight",
    )(w)


def _cast_pad_body(n_valid, n_blocks, x_ref, o_ref):
    is_pad = (pl.program_id(0) % n_blocks) >= n_valid
    o_ref[...] = jnp.where(is_pad, 0.0, x_ref[...]).astype(o_ref.dtype)


def cast_up_weight(w, layer, f, fp):
    d = w.shape[1]
    unit = int(np.gcd(f, fp))
    n_valid, n_blocks = f // unit, fp // unit

    def src(j):
        return (layer, 0, (j // n_blocks) * n_valid + jnp.minimum(j % n_blocks, n_valid - 1))

    return pl.pallas_call(
        functools.partial(_cast_pad_body, n_valid, n_blocks),
        grid=(2 * n_blocks,),
        in_specs=[pl.BlockSpec((None, d, unit), src)],
        out_specs=pl.BlockSpec((d, unit), lambda j: (0, j)),
        out_shape=jax.ShapeDtypeStruct((d, 2 * fp), BF16),
        compiler_params=_params("parallel"),
        name="cast_up_weight",
    )(w)


def cast_down_weight(w, layer, f, fp):
    d = w.shape[2]
    unit = int(np.gcd(f, fp))
    n_valid, n_blocks = f // unit, fp // unit
    return pl.pallas_call(
        functools.partial(_cast_pad_body, n_valid, n_blocks),
        grid=(n_blocks,),
        in_specs=[pl.BlockSpec((None, unit, d),
                               lambda i: (layer, jnp.minimum(i, n_valid - 1), 0))],
        out_specs=pl.BlockSpec((unit, d), lambda i: (i, 0)),
        out_shape=jax.ShapeDtypeStruct((fp, d), BF16),
        compiler_params=_params("parallel"),
        name="cast_down_weight",
    )(w)


def _rope_body(inv_ref, cos_ref, sin_ref):
    tt = cos_ref.shape[0]
    row = lax.broadcasted_iota(jnp.int32, (tt, LANES), 0) + pl.program_id(0) * tt
    lane = lax.broadcasted_iota(jnp.int32, (tt, LANES), 1)
    ang = row.astype(F32) * inv_ref[...]
    cos_ref[...] = jnp.cos(ang)
    sin_ref[...] = jnp.where(lane < HEAD_DIM // 2, -1.0, 1.0) * jnp.sin(ang)


def rope_tables(t):
    half = HEAD_DIM // 2
    inv = 1.0 / (ROPE_BASE ** jnp.linspace(0.0, 1.0, half, dtype=F32))
    inv2 = jnp.concatenate([inv, inv]).reshape(1, HEAD_DIM)
    tt = _tile(t, 512)
    return pl.pallas_call(
        _rope_body,
        grid=(t // tt,),
        in_specs=[pl.BlockSpec((1, HEAD_DIM), lambda i: (0, 0))],
        out_specs=[pl.BlockSpec((tt, HEAD_DIM), lambda i: (i, 0)),
                   pl.BlockSpec((tt, HEAD_DIM), lambda i: (i, 0))],
        out_shape=[jax.ShapeDtypeStruct((t, HEAD_DIM), F32)] * 2,
        compiler_params=_params("parallel"),
        name="rope_tables",
    )(inv2)


def _silu(x):
    return x * jax.nn.sigmoid(x)


def _retention_body(heads, logg_ref, q_ref, k_ref, v_ref, g_ref, cos_ref, sin_ref,
                    o_ref, state_ref):
    group = pl.program_id(1)
    step = pl.program_id(2)
    c = q_ref.shape[0]

    @pl.when(step == 0)
    def _():
        state_ref[...] = jnp.zeros_like(state_ref)

    cos = cos_ref[...]
    sin = sin_ref[...]

    def rot(x):
        return x * cos + pltpu.roll(x, HEAD_DIM // 2, axis=1) * sin

    row = lax.broadcasted_iota(jnp.int32, (c, c), 0)
    col = lax.broadcasted_iota(jnp.int32, (c, c), 1)
    diff = (row - col).astype(F32)
    causal = diff >= 0
    lag = jnp.maximum(diff, 0.0)
    pos = lax.broadcasted_iota(jnp.int32, (c, 1), 0).astype(F32)

    for hh in range(heads):
        lanes = slice(hh * HEAD_DIM, (hh + 1) * HEAD_DIM)
        lg = logg_ref[group * heads + hh]
        intra = jnp.where(causal, jnp.exp(lag * lg), 0.0)
        q_decay = jnp.exp((pos + 1.0) * lg)
        k_decay = jnp.exp((c - 1.0 - pos) * lg)
        block_decay = jnp.exp(jnp.full((1, HEAD_DIM), float(c), F32) * lg)

        q = rot(q_ref[:, lanes].astype(F32))
        k = rot(k_ref[:, lanes].astype(F32)) * (HEAD_DIM ** -0.5)
        v = v_ref[:, lanes]
        qb = q.astype(BF16)
        s = lax.dot_general(qb, k.astype(BF16), (((1,), (1,)), ((), ())),
                            preferred_element_type=F32) * intra
        state = state_ref[hh]
        o = jnp.dot(s.astype(BF16), v, preferred_element_type=F32)
        o += jnp.dot(qb, state.astype(BF16), preferred_element_type=F32) * q_decay
        kv = lax.dot_general((k * k_decay).astype(BF16), v, (((0,), (0,)), ((), ())),
                             preferred_element_type=F32)
        state_ref[hh] = state * block_decay + kv

        o = o * lax.rsqrt(jnp.mean(o * o, axis=-1, keepdims=True) + EPS)
        o_ref[:, lanes] = (_silu(g_ref[:, lanes].astype(F32)) * o).astype(o_ref.dtype)


def retention(proj, cos_t, sin_t, log_g, batch, t, n_heads, block):
    m = batch * t
    nt = t // block
    hd = HEAD_DIM
    heads = _tile(n_heads, HEAD_GROUP)
    ng = n_heads // heads

    def col(which):
        return pl.BlockSpec((block, heads * hd),
                            lambda b, g, i, lg: (b * nt + i, which * ng + g))

    tab = pl.BlockSpec((block, hd), lambda b, g, i, lg: (i, 0))
    return pl.pallas_call(
        functools.partial(_retention_body, heads),
        grid_spec=pltpu.PrefetchScalarGridSpec(
            num_scalar_prefetch=1,
            grid=(batch, ng, nt),
            in_specs=[col(0), col(1), col(2), col(3), tab, tab],
            out_specs=pl.BlockSpec((block, heads * hd), lambda b, g, i, lg: (b * nt + i, g)),
            scratch_shapes=[pltpu.VMEM((heads, hd, hd), F32)],
        ),
        out_shape=jax.ShapeDtypeStruct((m, n_heads * hd), BF16),
        compiler_params=_params("parallel", "parallel", "arbitrary"),
        name="retention",
    )(log_g, proj, proj, proj, proj, cos_t, sin_t)


def _att_bias_body(tab_ref, o_ref):
    qb, width = o_ref.shape[1], o_ref.shape[2]
    n = pl.cdiv(width + qb - 1, LANES) * LANES
    tab = jnp.broadcast_to(tab_ref[0], (SUBLANES, N_REL))
    u = lax.broadcasted_iota(jnp.int32, (N_REL, n), 1)
    kk = lax.broadcasted_iota(jnp.int32, (N_REL, n), 0)
    rel = jnp.clip(LEFT + qb - 1 - u, -(CHUNK - 1), REL_CLIP) + (CHUNK - 1)
    onehot = jnp.where(rel == kk, 1.0, 0.0).astype(BF16)
    e = jnp.zeros((SUBLANES, n), F32)
    rest = tab
    for _ in range(3):
        piece = rest.astype(BF16)
        e += jnp.dot(piece, onehot, preferred_element_type=F32)
        rest = rest - piece.astype(F32)
    eb = jnp.broadcast_to(e[0:1], (qb, n))
    bias = pltpu.roll(eb, n - (qb - 1), axis=1, stride=1, stride_axis=0)[:, :width]
    r = lax.broadcasted_iota(jnp.int32, (qb, width), 0) // CHUNK
    mc = lax.broadcasted_iota(jnp.int32, (qb, width), 1) // CHUNK
    in_band = (mc >= r) & (mc <= r + LEFT_CHUNKS)
    o_ref[0] = jnp.where(in_band, bias, MASKED)


def attention_bias(rel_table, qb):
    h = rel_table.shape[0]
    width = LEFT + qb
    return pl.pallas_call(
        _att_bias_body,
        grid=(h,),
        in_specs=[pl.BlockSpec((1, 1, N_REL), lambda i: (i, 0, 0))],
        out_specs=pl.BlockSpec((1, qb, width), lambda i: (i, 0, 0)),
        out_shape=jax.ShapeDtypeStruct((h, qb, width), F32),
        compiler_params=_params("parallel"),
        name="attention_bias",
    )(rel_table.reshape(h, 1, N_REL))


def _attention_body(heads, nkb, q_ref, *refs):
    k_refs, v_refs = refs[:nkb], refs[nkb:2 * nkb]
    bias_ref, o_ref = refs[2 * nkb], refs[2 * nkb + 1]
    step = pl.program_id(2)
    qb = q_ref.shape[0]
    scale = HEAD_DIM ** -0.5
    for hh in range(heads):
        lanes = slice(hh * HEAD_DIM, (hh + 1) * HEAD_DIM)
        q = q_ref[:, lanes]
        logits = []
        for j in range(nkb):
            s = lax.dot_general(q, k_refs[j][:, lanes], (((1,), (1,)), ((), ())),
                                preferred_element_type=F32)
            s = s * scale + bias_ref[hh, :, j * qb:(j + 1) * qb]
            if j < nkb - 1:
                s = jnp.where(step - (nkb - 1 - j) >= 0, s, MASKED)
            logits.append(s)
        mx = functools.reduce(jnp.maximum,
                              [jnp.max(s, axis=-1, keepdims=True) for s in logits])
        acc = jnp.zeros((qb, HEAD_DIM), F32)
        den = jnp.zeros((qb, 1), F32)
        for j in range(nkb):
            e = jnp.exp(logits[j] - mx)
            den += jnp.sum(e, axis=-1, keepdims=True)
            acc += jnp.dot(e.astype(BF16), v_refs[j][:, lanes], preferred_element_type=F32)
        o_ref[:, lanes] = (acc / den).astype(o_ref.dtype)


def band_attention(proj, bias, batch, t, n_heads, col0):
    m = batch * t
    qb = bias.shape[1]
    nkb = LEFT // qb + 1
    nq = t // qb
    hd = HEAD_DIM
    heads = _tile(n_heads, HEAD_GROUP)
    ng = n_heads // heads
    assert col0 % heads == 0
    cg0 = col0 // heads

    def kv_spec(which, j):
        back = nkb - 1 - j
        return pl.BlockSpec(
            (qb, heads * hd),
            lambda g, b, i: (b * nq + jnp.maximum(i - back, 0), cg0 + which * ng + g))

    in_specs = [pl.BlockSpec((qb, heads * hd), lambda g, b, i: (b * nq + i, cg0 + g))]
    in_specs += [kv_spec(1, j) for j in range(nkb)]
    in_specs += [kv_spec(2, j) for j in range(nkb)]
    in_specs += [pl.BlockSpec((heads, qb, LEFT + qb), lambda g, b, i: (g, 0, 0))]
    return pl.pallas_call(
        functools.partial(_attention_body, heads, nkb),
        grid=(ng, batch, nq),
        in_specs=in_specs,
        out_specs=pl.BlockSpec((qb, heads * hd), lambda g, b, i: (b * nq + i, g)),
        out_shape=jax.ShapeDtypeStruct((m, n_heads * hd), BF16),
        compiler_params=_params("parallel", "parallel", "arbitrary"),
        name="band_attention",
    )(*([proj] * (1 + 2 * nkb)), bias)


def _conv_act_body(tiles_per_seq, ug_ref, uv_ref, pg_ref, pv_ref, wg_ref, wv_ref,
                   bg_ref, bv_ref, o_ref, sg_ref, sv_ref):
    tt = ug_ref.shape[0]
    seq_start = (pl.program_id(1) % tiles_per_seq) == 0

    def conv(u_ref, p_ref, w_ref, b_ref, s_ref):
        s_ref[SUBLANES:, :] = u_ref[...].astype(F32)
        s_ref[0:SUBLANES, :] = jnp.where(seq_start, 0.0, p_ref[...].astype(F32))
        w = w_ref[...]
        out = b_ref[...] + w[CONV_WIDTH - 1:CONV_WIDTH] * s_ref[SUBLANES:, :]
        for d in range(1, CONV_WIDTH):
            tap = CONV_WIDTH - 1 - d
            out += w[tap:tap + 1] * s_ref[SUBLANES - d:SUBLANES - d + tt, :]
        return out

    g = conv(ug_ref, pg_ref, wg_ref, bg_ref, sg_ref)
    val = conv(uv_ref, pv_ref, wv_ref, bv_ref, sv_ref)
    o_ref[...] = (_silu(g) * val).astype(o_ref.dtype)


def conv_gate(u, conv_w, conv_b, t):
    m, f2 = u.shape
    f = f2 // 2
    tt, tn = _tile(t, 512), _tile(f, 512)
    nj = f // tn
    sub = tt // SUBLANES

    def cur(half):
        return pl.BlockSpec((tt, tn), lambda j, i: (i, half * nj + j))

    def prev(half):
        return pl.BlockSpec((SUBLANES, tn),
                            lambda j, i: (jnp.maximum(i * sub - 1, 0), half * nj + j))

    def wspec(half):
        return pl.BlockSpec((CONV_WIDTH, tn), lambda j, i: (0, half * nj + j))

    def bspec(half):
        return pl.BlockSpec((1, tn), lambda j, i: (0, half * nj + j))

    return pl.pallas_call(
        functools.partial(_conv_act_body, t // tt),
        grid=(nj, m // tt),
        in_specs=[cur(0), cur(1), prev(0), prev(1), wspec(0), wspec(1), bspec(0), bspec(1)],
        out_specs=pl.BlockSpec((tt, tn), lambda j, i: (i, j)),
        out_shape=jax.ShapeDtypeStruct((m, f), BF16),
        scratch_shapes=[pltpu.VMEM((SUBLANES + tt, tn), F32)] * 2,
        compiler_params=_params("parallel", "parallel"),
        name="conv_gate",
    )(u, u, u, u, conv_w, conv_w, conv_b.reshape(1, f2), conv_b.reshape(1, f2))


def _pad_ff_cols(w, f, fp):
    lead = w.shape[:-1]
    w = w.reshape(*lead, 2, f)
    w = jnp.pad(w, [(0, 0)] * len(lead) + [(0, 0), (0, fp - f)])
    return w.reshape(*lead, 2 * fp)


def kernel(x, ln_mix, w_in, rel_bias, w_out, ln_ffn, w_up, conv_w, conv_b, w_down, ln_final):
    batch, t, d = x.shape
    depth = w_in.shape[0]
    m = batch * t
    n_ret = (d // 2) // HEAD_DIM
    n_att = rel_bias.shape[1]
    f = w_down.shape[1]
    fp = -(-f // FF_ALIGN) * FF_ALIGN

    cos_t, sin_t = rope_tables(t)
    log_g = jnp.log(1.0 - 2.0 ** (-5.0 - jnp.arange(n_ret, dtype=F32)))
    ret_block = _tile(t, 256)
    att_block = _tile(LEFT, 256)
    conv_w_p = _pad_ff_cols(conv_w, f, fp)
    conv_b_p = _pad_ff_cols(conv_b, f, fp)

    h = x.reshape(m, d)
    for layer in range(depth):
        xn = rmsnorm(h, ln_mix[layer], BF16)
        proj = matmul(xn, cast_weight(w_in, layer), BF16, "in_proj")
        ro = retention(proj, cos_t, sin_t, log_g, batch, t, n_ret, ret_block)
        bias = attention_bias(rel_bias[layer], att_block)
        ao = band_attention(proj, bias, batch, t, n_att, 4 * n_ret)
        h = out_proj_residual(ro, ao, cast_weight(w_out, layer), h)

        xn = rmsnorm(h, ln_ffn[layer], BF16)
        u = matmul(xn, cast_up_weight(w_up, layer, f, fp), BF16, "up_proj")
        act = conv_gate(u, conv_w_p[layer], conv_b_p[layer], t)
        h = down_proj_residual(act, cast_down_weight(w_down, layer, f, fp), h)
    return rmsnorm(h, ln_final, F32).reshape(batch, t, d)
```

```python
import functools

import jax
import jax.numpy as jnp
from jax import lax
from jax.experimental import pallas as pl
from jax.experimental.pallas import tpu as pltpu

CHUNK = 64
HEAD_DIM = 128
LEFT_CHUNKS = 8
LEFT = LEFT_CHUNKS * CHUNK
REL_CLIP = 128
N_REL = REL_CLIP + CHUNK
CONV_WIDTH = 3
ROPE_BASE = 10000.0
EPS = 1e-6

LANES = 128
SUBLANES = 8
VMEM_LIMIT = 56 * 1024 * 1024
HEAD_GROUP = 4
FFN_EPILOGUE_ROWS = 64
FFN_K_CHUNK = 256
MASKED = -1e30

F32 = jnp.float32
BF16 = jnp.bfloat16


def _params(*sem):
    return pltpu.CompilerParams(dimension_semantics=sem, vmem_limit_bytes=VMEM_LIMIT)


def _tile(n, want):
    t = min(n, want)
    while n % t:
        t //= 2
    return t


def _rmsnorm_body(x_ref, g_ref, o_ref):
    x = x_ref[...]
    ms = jnp.mean(x * x, axis=-1, keepdims=True)
    o_ref[...] = (x * lax.rsqrt(ms + EPS) * g_ref[...]).astype(o_ref.dtype)


def rmsnorm(x, gain, out_dtype):
    m, d = x.shape
    tm = _tile(m, 256)
    return pl.pallas_call(
        _rmsnorm_body,
        grid=(m // tm,),
        in_specs=[pl.BlockSpec((tm, d), lambda i: (i, 0)),
                  pl.BlockSpec((1, d), lambda i: (0, 0))],
        out_specs=pl.BlockSpec((tm, d), lambda i: (i, 0)),
        out_shape=jax.ShapeDtypeStruct((m, d), out_dtype),
        compiler_params=_params("parallel"),
        name="rmsnorm",
    )(x, gain.reshape(1, d))


def _mm_body(a_ref, b_ref, o_ref):
    o_ref[...] = jnp.dot(a_ref[...], b_ref[...],
                         preferred_element_type=F32).astype(o_ref.dtype)


def matmul(a, b, out_dtype, name):
    m, k = a.shape
    n = b.shape[1]
    tm, tn = _tile(m, 1024), _tile(n, 1024)
    return pl.pallas_call(
        _mm_body,
        grid=(m // tm, n // tn),
        in_specs=[pl.BlockSpec((tm, k), lambda i, j: (i, 0)),
                  pl.BlockSpec((k, tn), lambda i, j: (0, j))],
        out_specs=pl.BlockSpec((tm, tn), lambda i, j: (i, j)),
        out_shape=jax.ShapeDtypeStruct((m, n), out_dtype),
        compiler_params=_params("parallel", "arbitrary"),
        name=name,
    )(a, b)


def _mm2_res_body(a1_ref, a2_ref, b1_ref, b2_ref, r_ref, o_ref):
    acc = jnp.dot(a1_ref[...], b1_ref[...], preferred_element_type=F32)
    acc += jnp.dot(a2_ref[...], b2_ref[...], preferred_element_type=F32)
    o_ref[...] = r_ref[...] + acc


def out_proj_residual(a1, a2, w, res):
    m, k1 = a1.shape
    k2 = a2.shape[1]
    n = w.shape[1]
    assert k1 == k2 and w.shape[0] == k1 + k2
    tm, tn = _tile(m, 512), _tile(n, 1024)
    return pl.pallas_call(
        _mm2_res_body,
        grid=(m // tm, n // tn),
        in_specs=[pl.BlockSpec((tm, k1), lambda i, j: (i, 0)),
                  pl.BlockSpec((tm, k2), lambda i, j: (i, 0)),
                  pl.BlockSpec((k1, tn), lambda i, j: (0, j)),
                  pl.BlockSpec((k2, tn), lambda i, j: (1, j)),
                  pl.BlockSpec((tm, tn), lambda i, j: (i, j))],
        out_specs=pl.BlockSpec((tm, tn), lambda i, j: (i, j)),
        out_shape=jax.ShapeDtypeStruct((m, n), F32),
        compiler_params=_params("parallel", "arbitrary"),
        name="out_proj",
    )(a1, a2, w, w, res)


def _mm_res_k_body(nk, k_last, a_ref, b_ref, r_ref, o_ref, acc_ref):
    k = pl.program_id(2)

    @pl.when(k == 0)
    def _():
        acc_ref[...] = jnp.dot(a_ref[...], b_ref[...], preferred_element_type=F32)

    @pl.when((k > 0) & (k < nk - 1))
    def _():
        acc_ref[...] += jnp.dot(a_ref[...], b_ref[...], preferred_element_type=F32)

    @pl.when(k == nk - 1)
    def _():
        tail = jnp.dot(a_ref[:, :k_last], b_ref[:k_last, :], preferred_element_type=F32)
        o_ref[...] = r_ref[...] + (acc_ref[...] + tail)


def down_proj_residual(a, w, res):
    m, k = a.shape
    n = w.shape[1]
    tm, tn = _tile(m, 1024), _tile(n, 1024)
    tk = -(-k // (4 * 2 * LANES)) * 2 * LANES
    nk = -(-k // tk)
    assert nk >= 2
    k_last = k - (nk - 1) * tk
    return pl.pallas_call(
        functools.partial(_mm_res_k_body, nk, k_last),
        grid=(m // tm, n // tn, nk),
        in_specs=[pl.BlockSpec((tm, tk), lambda i, j, kk: (i, kk)),
                  pl.BlockSpec((tk, tn), lambda i, j, kk: (kk, j)),
                  pl.BlockSpec((tm, tn), lambda i, j, kk: (i, j))],
        out_specs=pl.BlockSpec((tm, tn), lambda i, j, kk: (i, j)),
        out_shape=jax.ShapeDtypeStruct((m, n), F32),
        scratch_shapes=[pltpu.VMEM((tm, tn), F32)],
        compiler_params=_params("parallel", "arbitrary", "arbitrary"),
        name="down_proj",
    )(a, w, res)


def _cast_body(x_ref, o_ref):
    o_ref[...] = x_ref[...].astype(o_ref.dtype)


def cast_weight(w, layer):
    _, k, n = w.shape
    tk, tn = _tile(k, 1024), _tile(n, 2048)
    return pl.pallas_call(
        _cast_body,
        grid=(k // tk, n // tn),
        in_specs=[pl.BlockSpec((None, tk, tn), lambda i, j: (layer, i, j))],
        out_specs=pl.BlockSpec((tk, tn), lambda i, j: (i, j)),
        out_shape=jax.ShapeDtypeStruct((k, n), BF16),
        compiler_params=_params("parallel", "parallel"),
        name="cast_weight",
    )(w)


def _rope_body(inv_ref, cos_ref, sin_ref):
    tt = cos_ref.shape[0]
    row = lax.broadcasted_iota(jnp.int32, (tt, LANES), 0) + pl.program_id(0) * tt
    lane = lax.broadcasted_iota(jnp.int32, (tt, LANES), 1)
    ang = row.astype(F32) * inv_ref[...]
    cos_ref[...] = jnp.cos(ang)
    sin_ref[...] = jnp.where(lane < HEAD_DIM // 2, -1.0, 1.0) * jnp.sin(ang)


def rope_tables(t):
    half = HEAD_DIM // 2
    inv = 1.0 / (ROPE_BASE ** jnp.linspace(0.0, 1.0, half, dtype=F32))
    inv2 = jnp.concatenate([inv, inv]).reshape(1, HEAD_DIM)
    tt = _tile(t, 512)
    return pl.pallas_call(
        _rope_body,
        grid=(t // tt,),
        in_specs=[pl.BlockSpec((1, HEAD_DIM), lambda i: (0, 0))],
        out_specs=[pl.BlockSpec((tt, HEAD_DIM), lambda i: (i, 0)),
                   pl.BlockSpec((tt, HEAD_DIM), lambda i: (i, 0))],
        out_shape=[jax.ShapeDtypeStruct((t, HEAD_DIM), F32)] * 2,
        compiler_params=_params("parallel"),
        name="rope_tables",
    )(inv2)


def _silu(x):
    return x * jax.nn.sigmoid(x)


def _retention_body(heads, logg_ref, q_ref, k_ref, v_ref, g_ref, cos_ref, sin_ref,
                    o_ref, state_ref):
    group = pl.program_id(1)
    step = pl.program_id(2)
    c = q_ref.shape[0]

    @pl.when(step == 0)
    def _():
        state_ref[...] = jnp.zeros_like(state_ref)

    cos = cos_ref[...]
    sin = sin_ref[...]

    def rot(x):
        return x * cos + pltpu.roll(x, HEAD_DIM // 2, axis=1) * sin

    row = lax.broadcasted_iota(jnp.int32, (c, c), 0)
    col = lax.broadcasted_iota(jnp.int32, (c, c), 1)
    diff = (row - col).astype(F32)
    causal = diff >= 0
    lag = jnp.maximum(diff, 0.0)
    pos = lax.broadcasted_iota(jnp.int32, (c, 1), 0).astype(F32)

    for hh in range(heads):
        lanes = slice(hh * HEAD_DIM, (hh + 1) * HEAD_DIM)
        lg = logg_ref[group * heads + hh]
        intra = jnp.where(causal, jnp.exp(lag * lg), 0.0)
        q_decay = jnp.exp((pos + 1.0) * lg)
        k_decay = jnp.exp((c - 1.0 - pos) * lg)
        block_decay = jnp.exp(jnp.full((1, HEAD_DIM), float(c), F32) * lg)

        q = rot(q_ref[:, lanes].astype(F32))
        k = rot(k_ref[:, lanes].astype(F32)) * (HEAD_DIM ** -0.5)
        v = v_ref[:, lanes]
        qb = q.astype(BF16)
        s = lax.dot_general(qb, k.astype(BF16), (((1,), (1,)), ((), ())),
                            preferred_element_type=F32) * intra
        state = state_ref[hh]
        o = jnp.dot(s.astype(BF16), v, preferred_element_type=F32)
        o += jnp.dot(qb, state.astype(BF16), preferred_element_type=F32) * q_decay
        kv = lax.dot_general((k * k_decay).astype(BF16), v, (((0,), (0,)), ((), ())),
                             preferred_element_type=F32)
        state_ref[hh] = state * block_decay + kv

        o = o * lax.rsqrt(jnp.mean(o * o, axis=-1, keepdims=True) + EPS)
        o_ref[:, lanes] = (_silu(g_ref[:, lanes].astype(F32)) * o).astype(o_ref.dtype)


def retention(proj, cos_t, sin_t, log_g, batch, t, n_heads, block):
    m = batch * t
    nt = t // block
    hd = HEAD_DIM
    heads = _tile(n_heads, HEAD_GROUP)
    ng = n_heads // heads

    def col(which):
        return pl.BlockSpec((block, heads * hd),
                            lambda b, g, i, lg: (b * nt + i, which * ng + g))

    tab = pl.BlockSpec((block, hd), lambda b, g, i, lg: (i, 0))
    return pl.pallas_call(
        functools.partial(_retention_body, heads),
        grid_spec=pltpu.PrefetchScalarGridSpec(
            num_scalar_prefetch=1,
            grid=(batch, ng, nt),
            in_specs=[col(0), col(1), col(2), col(3), tab, tab],
            out_specs=pl.BlockSpec((block, heads * hd), lambda b, g, i, lg: (b * nt + i, g)),
            scratch_shapes=[pltpu.VMEM((heads, hd, hd), F32)],
        ),
        out_shape=jax.ShapeDtypeStruct((m, n_heads * hd), BF16),
        compiler_params=_params("parallel", "parallel", "arbitrary"),
        name="retention",
    )(log_g, proj, proj, proj, proj, cos_t, sin_t)


def _att_bias_body(tab_ref, o_ref):
    qb, width = o_ref.shape[1], o_ref.shape[2]
    n = pl.cdiv(width + qb - 1, LANES) * LANES
    tab = jnp.broadcast_to(tab_ref[0], (SUBLANES, N_REL))
    u = lax.broadcasted_iota(jnp.int32, (N_REL, n), 1)
    kk = lax.broadcasted_iota(jnp.int32, (N_REL, n), 0)
    rel = jnp.clip(LEFT + qb - 1 - u, -(CHUNK - 1), REL_CLIP) + (CHUNK - 1)
    onehot = jnp.where(rel == kk, 1.0, 0.0).astype(BF16)
    e = jnp.zeros((SUBLANES, n), F32)
    rest = tab
    for _ in range(3):
        piece = rest.astype(BF16)
        e += jnp.dot(piece, onehot, preferred_element_type=F32)
        rest = rest - piece.astype(F32)
    eb = jnp.broadcast_to(e[0:1], (qb, n))
    bias = pltpu.roll(eb, n - (qb - 1), axis=1, stride=1, stride_axis=0)[:, :width]
    r = lax.broadcasted_iota(jnp.int32, (qb, width), 0) // CHUNK
    mc = lax.broadcasted_iota(jnp.int32, (qb, width), 1) // CHUNK
    in_band = (mc >= r) & (mc <= r + LEFT_CHUNKS)
    o_ref[0] = jnp.where(in_band, bias, MASKED)


def attention_bias(rel_table, qb):
    h = rel_table.shape[0]
    width = LEFT + qb
    return pl.pallas_call(
        _att_bias_body,
        grid=(h,),
        in_specs=[pl.BlockSpec((1, 1, N_REL), lambda i: (i, 0, 0))],
        out_specs=pl.BlockSpec((1, qb, width), lambda i: (i, 0, 0)),
        out_shape=jax.ShapeDtypeStruct((h, qb, width), F32),
        compiler_params=_params("parallel"),
        name="attention_bias",
    )(rel_table.reshape(h, 1, N_REL))


def _attention_body(heads, nkb, q_ref, *refs):
    k_refs, v_refs = refs[:nkb], refs[nkb:2 * nkb]
    bias_ref, o_ref = refs[2 * nkb], refs[2 * nkb + 1]
    step = pl.program_id(2)
    qb = q_ref.shape[0]
    scale = HEAD_DIM ** -0.5
    for hh in range(heads):
        lanes = slice(hh * HEAD_DIM, (hh + 1) * HEAD_DIM)
        q = q_ref[:, lanes]
        logits = []
        for j in range(nkb):
            s = lax.dot_general(q, k_refs[j][:, lanes], (((1,), (1,)), ((), ())),
                                preferred_element_type=F32)
            s = s * scale + bias_ref[hh, :, j * qb:(j + 1) * qb]
            if j < nkb - 1:
                s = jnp.where(step - (nkb - 1 - j) >= 0, s, MASKED)
            logits.append(s)
        mx = functools.reduce(jnp.maximum,
                              [jnp.max(s, axis=-1, keepdims=True) for s in logits])
        acc = jnp.zeros((qb, HEAD_DIM), F32)
        den = jnp.zeros((qb, 1), F32)
        for j in range(nkb):
            e = jnp.exp(logits[j] - mx)
            den += jnp.sum(e, axis=-1, keepdims=True)
            acc += jnp.dot(e.astype(BF16), v_refs[j][:, lanes], preferred_element_type=F32)
        o_ref[:, lanes] = (acc / den).astype(o_ref.dtype)


def band_attention(proj, bias, batch, t, n_heads, col0):
    m = batch * t
    qb = bias.shape[1]
    nkb = LEFT // qb + 1
    nq = t // qb
    hd = HEAD_DIM
    heads = _tile(n_heads, HEAD_GROUP)
    ng = n_heads // heads
    assert col0 % heads == 0
    cg0 = col0 // heads

    def kv_spec(which, j):
        back = nkb - 1 - j
        return pl.BlockSpec(
            (qb, heads * hd),
            lambda g, b, i: (b * nq + jnp.maximum(i - back, 0), cg0 + which * ng + g))

    in_specs = [pl.BlockSpec((qb, heads * hd), lambda g, b, i: (b * nq + i, cg0 + g))]
    in_specs += [kv_spec(1, j) for j in range(nkb)]
    in_specs += [kv_spec(2, j) for j in range(nkb)]
    in_specs += [pl.BlockSpec((heads, qb, LEFT + qb), lambda g, b, i: (g, 0, 0))]
    return pl.pallas_call(
        functools.partial(_attention_body, heads, nkb),
        grid=(ng, batch, nq),
        in_specs=in_specs,
        out_specs=pl.BlockSpec((qb, heads * hd), lambda g, b, i: (b * nq + i, g)),
        out_shape=jax.ShapeDtypeStruct((m, n_heads * hd), BF16),
        compiler_params=_params("parallel", "parallel", "arbitrary"),
        name="band_attention",
    )(*([proj] * (1 + 2 * nkb)), bias)


def _ffn_up_step(tiles_per_seq, nj, a_ref, wg_ref, wv_ref, cwg_ref, cwv_ref, cbg_ref, cbv_ref,
                 o_ref, cur_ref, prev_ref, halo_ref):
    tm = a_ref.shape[0]
    prev = jnp.maximum(pl.program_id(0) - 1, 0)
    jp = prev % nj
    seq_start = ((prev // nj) % tiles_per_seq) == 0

    above = [halo_ref[half, jp] for half in range(2)]
    for half in range(2):
        halo_ref[half, jp] = prev_ref[half, tm:, :]

    taps, biases = [], []
    for half, (w_ref, b_ref) in enumerate(((cwg_ref, cbg_ref), (cwv_ref, cbv_ref))):
        prev_ref[half, 0:SUBLANES, :] = jnp.where(seq_start, 0.0, above[half])
        w = w_ref[...]
        taps.append([w[k:k + 1] for k in range(CONV_WIDTH)])
        biases.append(b_ref[...])

    def conv(half, r0, rows):
        out = biases[half]
        for k in range(CONV_WIDTH):
            lo = SUBLANES + r0 - (CONV_WIDTH - 1 - k)
            out = out + taps[half][k] * prev_ref[half, lo:lo + rows, :]
        return out

    d = a_ref.shape[1]
    n_chunks = max(1, min(tm // FFN_EPILOGUE_ROWS, d // FFN_K_CHUNK))
    rows, kc = tm // n_chunks, d // n_chunks
    acc = [None, None]
    for c in range(n_chunks):
        a = a_ref[:, c * kc:(c + 1) * kc]
        for half, w_ref in enumerate((wg_ref, wv_ref)):
            part = jnp.dot(a, w_ref[c * kc:(c + 1) * kc, :], preferred_element_type=F32)
            acc[half] = part if acc[half] is None else acc[half] + part
        r0 = c * rows
        g = conv(0, r0, rows)
        val = conv(1, r0, rows)
        o_ref[r0:r0 + rows, :] = (_silu(g) * val).astype(o_ref.dtype)
    cur_ref[0, SUBLANES:, :] = acc[0]
    cur_ref[1, SUBLANES:, :] = acc[1]


def _ffn_up_body(tiles_per_seq, nj, *refs):
    io_refs, (raw_a, raw_b, halo_ref) = refs[:-3], refs[-3:]
    step = pl.program_id(0)
    run = functools.partial(_ffn_up_step, tiles_per_seq, nj, *io_refs)

    @pl.when(step == 0)
    def _():
        raw_b[...] = jnp.zeros_like(raw_b)
        halo_ref[...] = jnp.zeros_like(halo_ref)

    @pl.when(step % 2 == 0)
    def _():
        run(raw_a, raw_b, halo_ref)

    @pl.when(step % 2 == 1)
    def _():
        run(raw_b, raw_a, halo_ref)


def ffn_up(xn, w, conv_w, conv_b, t):
    m, d = xn.shape
    f = w.shape[1] // 2
    tm, tn = _tile(min(m, t), 1024), _tile(f, 2 * LANES)
    ni, nj = m // tm, f // tn
    n_tiles = ni * nj

    def cur(s):
        c = jnp.minimum(s, n_tiles - 1)
        return c // nj, c % nj

    def prev(s):
        p = jnp.maximum(s - 1, 0)
        return p // nj, p % nj

    def wspec(half):
        return pl.BlockSpec((d, tn), lambda s: (0, half * nj + cur(s)[1]))

    def cspec(rows, half):
        return pl.BlockSpec((rows, tn), lambda s: (0, half * nj + prev(s)[1]))

    raw = pltpu.VMEM((2, SUBLANES + tm, tn), F32)
    return pl.pallas_call(
        functools.partial(_ffn_up_body, t // tm, nj),
        grid=(n_tiles + 1,),
        in_specs=[pl.BlockSpec((tm, d), lambda s: (cur(s)[0], 0)), wspec(0), wspec(1),
                  cspec(CONV_WIDTH, 0), cspec(CONV_WIDTH, 1), cspec(1, 0), cspec(1, 1)],
        out_specs=pl.BlockSpec((tm, tn), lambda s: prev(s)),
        out_shape=jax.ShapeDtypeStruct((m, f), BF16),
        scratch_shapes=[raw, raw, pltpu.VMEM((2, nj, SUBLANES, tn), F32)],
        compiler_params=_params("arbitrary"),
        name="ffn_up",
    )(xn, w, w, conv_w, conv_w, conv_b.reshape(1, 2 * f), conv_b.reshape(1, 2 * f))


def kernel(x, ln_mix, w_in, rel_bias, w_out, ln_ffn, w_up, conv_w, conv_b, w_down, ln_final):
    batch, t, d = x.shape
    depth = w_in.shape[0]
    m = batch * t
    n_ret = (d // 2) // HEAD_DIM
    n_att = rel_bias.shape[1]

    cos_t, sin_t = rope_tables(t)
    log_g = jnp.log(1.0 - 2.0 ** (-5.0 - jnp.arange(n_ret, dtype=F32)))
    ret_block = _tile(t, 256)
    att_block = _tile(LEFT, 256)

    h = x.reshape(m, d)
    for layer in range(depth):
        xn = rmsnorm(h, ln_mix[layer], BF16)
        proj = matmul(xn, cast_weight(w_in, layer), BF16, "in_proj")
        ro = retention(proj, cos_t, sin_t, log_g, batch, t, n_ret, ret_block)
        bias = attention_bias(rel_bias[layer], att_block)
        ao = band_attention(proj, bias, batch, t, n_att, 4 * n_ret)
        h = out_proj_residual(ro, ao, cast_weight(w_out, layer), h)

        xn = rmsnorm(h, ln_ffn[layer], BF16)
        act = ffn_up(xn, cast_weight(w_up, layer), conv_w[layer], conv_b[layer], t)
        h = down_proj_residual(act, cast_weight(w_down, layer), h)
    return rmsnorm(h, ln_final, F32).reshape(batch, t, d)
```

```python
import functools

import jax
import jax.numpy as jnp
from jax import lax
from jax.experimental import pallas as pl
from jax.experimental.pallas import tpu as pltpu

CHUNK = 64
HEAD_DIM = 128
LEFT_CHUNKS = 8
LEFT = LEFT_CHUNKS * CHUNK
REL_CLIP = 128
N_REL = REL_CLIP + CHUNK
CONV_WIDTH = 3
ROPE_BASE = 10000.0
EPS = 1e-6

LANES = 128
SUBLANES = 8
VMEM_LIMIT = 56 * 1024 * 1024
HEAD_GROUP = 4
FFN_EPILOGUE_ROWS = 64
FFN_K_CHUNK = 256
MASKED = -1e30

F32 = jnp.float32
BF16 = jnp.bfloat16


def _params(*sem):
    return pltpu.CompilerParams(dimension_semantics=sem, vmem_limit_bytes=VMEM_LIMIT)


def _tile(n, want):
    t = min(n, want)
    while n % t:
        t //= 2
    return t


def _rmsnorm_body(x_ref, g_ref, o_ref):
    x = x_ref[...]
    ms = jnp.mean(x * x, axis=-1, keepdims=True)
    o_ref[...] = (x * lax.rsqrt(ms + EPS) * g_ref[...]).astype(o_ref.dtype)


def rmsnorm(x, gain, out_dtype):
    m, d = x.shape
    tm = _tile(m, 256)
    return pl.pallas_call(
        _rmsnorm_body,
        grid=(m // tm,),
        in_specs=[pl.BlockSpec((tm, d), lambda i: (i, 0)),
                  pl.BlockSpec((1, d), lambda i: (0, 0))],
        out_specs=pl.BlockSpec((tm, d), lambda i: (i, 0)),
        out_shape=jax.ShapeDtypeStruct((m, d), out_dtype),
        compiler_params=_params("parallel"),
        name="rmsnorm",
    )(x, gain.reshape(1, d))


def _sum_squares(x):
    return jnp.sum(x * x, axis=-1, keepdims=True)


def _row_scale(ss_ref, d):
    return lax.rsqrt(jnp.sum(ss_ref[...], axis=0) / d + EPS)


def _ss_in_spec(ss, tm, row_of):
    return pl.BlockSpec((ss.shape[0], tm, 1), lambda *idx: (0, row_of(*idx), 0))


def _prescale_body(x_ref, g_ref, o_ref, ss_ref):
    x = x_ref[...]
    o_ref[...] = (x * g_ref[...]).astype(o_ref.dtype)
    ss_ref[...] = _sum_squares(x)


def prescale(x, gain):
    m, d = x.shape
    tm = _tile(m, 256)
    return pl.pallas_call(
        _prescale_body,
        grid=(m // tm,),
        in_specs=[pl.BlockSpec((tm, d), lambda i: (i, 0)),
                  pl.BlockSpec((1, d), lambda i: (0, 0))],
        out_specs=[pl.BlockSpec((tm, d), lambda i: (i, 0)),
                   pl.BlockSpec((None, tm, 1), lambda i: (0, i, 0))],
        out_shape=[jax.ShapeDtypeStruct((m, d), BF16),
                   jax.ShapeDtypeStruct((1, m, 1), F32)],
        compiler_params=_params("parallel"),
        name="prescale",
    )(x, gain.reshape(1, d))


def _in_proj_body(a_ref, ss_ref, b_ref, o_ref):
    acc = jnp.dot(a_ref[...], b_ref[...], preferred_element_type=F32)
    o_ref[...] = (acc * _row_scale(ss_ref, a_ref.shape[1])).astype(o_ref.dtype)


def in_proj(a, ss, b):
    m, k = a.shape
    n = b.shape[1]
    tm, tn = _tile(m, 1024), _tile(n, 1024)
    return pl.pallas_call(
        _in_proj_body,
        grid=(m // tm, n // tn),
        in_specs=[pl.BlockSpec((tm, k), lambda i, j: (i, 0)),
                  _ss_in_spec(ss, tm, lambda i, j: i),
                  pl.BlockSpec((k, tn), lambda i, j: (0, j))],
        out_specs=pl.BlockSpec((tm, tn), lambda i, j: (i, j)),
        out_shape=jax.ShapeDtypeStruct((m, n), BF16),
        compiler_params=_params("parallel", "arbitrary"),
        name="in_proj",
    )(a, ss, b)


def _emit_residual(h, g_ref, o_ref, hg_ref, ss_ref):
    o_ref[...] = h
    hg_ref[...] = (h * g_ref[...]).astype(hg_ref.dtype)
    ss_ref[...] = _sum_squares(h)


def _residual_out(m, n, tm, tn, idx):
    specs = [pl.BlockSpec((tm, tn), lambda *g: idx(*g)),
             pl.BlockSpec((tm, tn), lambda *g: idx(*g)),
             pl.BlockSpec((None, tm, 1), lambda *g: (idx(*g)[1], idx(*g)[0], 0))]
    shapes = [jax.ShapeDtypeStruct((m, n), F32), jax.ShapeDtypeStruct((m, n), BF16),
              jax.ShapeDtypeStruct((n // tn, m, 1), F32)]
    return specs, shapes


def _out_proj_body(a1_ref, a2_ref, b1_ref, b2_ref, r_ref, g_ref, o_ref, hg_ref, ss_ref):
    acc = jnp.dot(a1_ref[...], b1_ref[...], preferred_element_type=F32)
    acc += jnp.dot(a2_ref[...], b2_ref[...], preferred_element_type=F32)
    _emit_residual(r_ref[...] + acc, g_ref, o_ref, hg_ref, ss_ref)


def out_proj_residual(a1, a2, w, res, next_gain):
    m, k1 = a1.shape
    k2 = a2.shape[1]
    n = w.shape[1]
    assert k1 == k2 and w.shape[0] == k1 + k2
    tm, tn = _tile(m, 512), _tile(n, 1024)
    out_specs, out_shape = _residual_out(m, n, tm, tn, lambda i, j: (i, j))
    return pl.pallas_call(
        _out_proj_body,
        grid=(m // tm, n // tn),
        in_specs=[pl.BlockSpec((tm, k1), lambda i, j: (i, 0)),
                  pl.BlockSpec((tm, k2), lambda i, j: (i, 0)),
                  pl.BlockSpec((k1, tn), lambda i, j: (0, j)),
                  pl.BlockSpec((k2, tn), lambda i, j: (1, j)),
                  pl.BlockSpec((tm, tn), lambda i, j: (i, j)),
                  pl.BlockSpec((1, tn), lambda i, j: (0, j))],
        out_specs=out_specs,
        out_shape=out_shape,
        compiler_params=_params("parallel", "arbitrary"),
        name="out_proj",
    )(a1, a2, w, w, res, next_gain.reshape(1, n))


def _down_proj_body(nk, k_last, emit_next, a_ref, b_ref, r_ref, *refs):
    k = pl.program_id(2)
    acc_ref = refs[-1]

    @pl.when(k == 0)
    def _():
        acc_ref[...] = jnp.dot(a_ref[...], b_ref[...], preferred_element_type=F32)

    @pl.when((k > 0) & (k < nk - 1))
    def _():
        acc_ref[...] += jnp.dot(a_ref[...], b_ref[...], preferred_element_type=F32)

    @pl.when(k == nk - 1)
    def _():
        tail = jnp.dot(a_ref[:, :k_last], b_ref[:k_last, :], preferred_element_type=F32)
        h = r_ref[...] + (acc_ref[...] + tail)
        if emit_next:
            _emit_residual(h, *refs[:4])
        else:
            refs[0][...] = h


def down_proj_residual(a, w, res, next_gain=None):
    m, k = a.shape
    n = w.shape[1]
    tm, tn = _tile(m, 1024), _tile(n, 1024)
    tk = -(-k // (4 * 2 * LANES)) * 2 * LANES
    nk = -(-k // tk)
    assert nk >= 2
    k_last = k - (nk - 1) * tk
    emit_next = next_gain is not None
    in_specs = [pl.BlockSpec((tm, tk), lambda i, j, kk: (i, kk)),
                pl.BlockSpec((tk, tn), lambda i, j, kk: (kk, j)),
                pl.BlockSpec((tm, tn), lambda i, j, kk: (i, j))]
    args = [a, w, res]
    out_specs, out_shape = _residual_out(m, n, tm, tn, lambda i, j, kk: (i, j))
    if emit_next:
        in_specs.append(pl.BlockSpec((1, tn), lambda i, j, kk: (0, j)))
        args.append(next_gain.reshape(1, n))
    else:
        out_specs, out_shape = out_specs[0], out_shape[0]
    return pl.pallas_call(
        functools.partial(_down_proj_body, nk, k_last, emit_next),
        grid=(m // tm, n // tn, nk),
        in_specs=in_specs,
        out_specs=out_specs,
        out_shape=out_shape,
        scratch_shapes=[pltpu.VMEM((tm, tn), F32)],
        compiler_params=_params("parallel", "arbitrary", "arbitrary"),
        name="down_proj",
    )(*args)


def _cast_body(x_ref, o_ref):
    o_ref[...] = x_ref[...].astype(o_ref.dtype)


def cast_weight(w, layer):
    _, k, n = w.shape
    tk, tn = _tile(k, 1024), _tile(n, 2048)
    return pl.pallas_call(
        _cast_body,
        grid=(k // tk, n // tn),
        in_specs=[pl.BlockSpec((None, tk, tn), lambda i, j: (layer, i, j))],
        out_specs=pl.BlockSpec((tk, tn), lambda i, j: (i, j)),
        out_shape=jax.ShapeDtypeStruct((k, n), BF16),
        compiler_params=_params("parallel", "parallel"),
        name="cast_weight",
    )(w)


def _rope_body(inv_ref, cos_ref, sin_ref):
    tt = cos_ref.shape[0]
    row = lax.broadcasted_iota(jnp.int32, (tt, LANES), 0) + pl.program_id(0) * tt
    lane = lax.broadcasted_iota(jnp.int32, (tt, LANES), 1)
    ang = row.astype(F32) * inv_ref[...]
    cos_ref[...] = jnp.cos(ang)
    sin_ref[...] = jnp.where(lane < HEAD_DIM // 2, -1.0, 1.0) * jnp.sin(ang)


def rope_tables(t):
    half = HEAD_DIM // 2
    inv = 1.0 / (ROPE_BASE ** jnp.linspace(0.0, 1.0, half, dtype=F32))
    inv2 = jnp.concatenate([inv, inv]).reshape(1, HEAD_DIM)
    tt = _tile(t, 512)
    return pl.pallas_call(
        _rope_body,
        grid=(t // tt,),
        in_specs=[pl.BlockSpec((1, HEAD_DIM), lambda i: (0, 0))],
        out_specs=[pl.BlockSpec((tt, HEAD_DIM), lambda i: (i, 0)),
                   pl.BlockSpec((tt, HEAD_DIM), lambda i: (i, 0))],
        out_shape=[jax.ShapeDtypeStruct((t, HEAD_DIM), F32)] * 2,
        compiler_params=_params("parallel"),
        name="rope_tables",
    )(inv2)


def _silu(x):
    return x * jax.nn.sigmoid(x)


def _retention_body(heads, logg_ref, q_ref, k_ref, v_ref, g_ref, cos_ref, sin_ref,
                    o_ref, state_ref):
    group = pl.program_id(1)
    step = pl.program_id(2)
    c = q_ref.shape[0]

    @pl.when(step == 0)
    def _():
        state_ref[...] = jnp.zeros_like(state_ref)

    cos = cos_ref[...]
    sin = sin_ref[...]

    def rot(x):
        return x * cos + pltpu.roll(x, HEAD_DIM // 2, axis=1) * sin

    row = lax.broadcasted_iota(jnp.int32, (c, c), 0)
    col = lax.broadcasted_iota(jnp.int32, (c, c), 1)
    diff = (row - col).astype(F32)
    causal = diff >= 0
    lag = jnp.maximum(diff, 0.0)
    pos = lax.broadcasted_iota(jnp.int32, (c, 1), 0).astype(F32)

    for hh in range(heads):
        lanes = slice(hh * HEAD_DIM, (hh + 1) * HEAD_DIM)
        lg = logg_ref[group * heads + hh]
        intra = jnp.where(causal, jnp.exp(lag * lg), 0.0)
        q_decay = jnp.exp((pos + 1.0) * lg)
        k_decay = jnp.exp((c - 1.0 - pos) * lg)
        block_decay = jnp.exp(jnp.full((1, HEAD_DIM), float(c), F32) * lg)

        q = rot(q_ref[:, lanes].astype(F32))
        k = rot(k_ref[:, lanes].astype(F32)) * (HEAD_DIM ** -0.5)
        v = v_ref[:, lanes]
        qb = q.astype(BF16)
        s = lax.dot_general(qb, k.astype(BF16), (((1,), (1,)), ((), ())),
                            preferred_element_type=F32) * intra
        state = state_ref[hh]
        o = jnp.dot(s.astype(BF16), v, preferred_element_type=F32)
        o += jnp.dot(qb, state.astype(BF16), preferred_element_type=F32) * q_decay
        kv = lax.dot_general((k * k_decay).astype(BF16), v, (((0,), (0,)), ((), ())),
                             preferred_element_type=F32)
        state_ref[hh] = state * block_decay + kv

        o = o * lax.rsqrt(jnp.mean(o * o, axis=-1, keepdims=True) + EPS)
        o_ref[:, lanes] = (_silu(g_ref[:, lanes].astype(F32)) * o).astype(o_ref.dtype)


def retention(proj, cos_t, sin_t, log_g, batch, t, n_heads, block):
    m = batch * t
    nt = t // block
    hd = HEAD_DIM
    heads = _tile(n_heads, HEAD_GROUP)
    ng = n_heads // heads

    def col(which):
        return pl.BlockSpec((block, heads * hd),
                            lambda b, g, i, lg: (b * nt + i, which * ng + g))

    tab = pl.BlockSpec((block, hd), lambda b, g, i, lg: (i, 0))
    return pl.pallas_call(
        functools.partial(_retention_body, heads),
        grid_spec=pltpu.PrefetchScalarGridSpec(
            num_scalar_prefetch=1,
            grid=(batch, ng, nt),
            in_specs=[col(0), col(1), col(2), col(3), tab, tab],
            out_specs=pl.BlockSpec((block, heads * hd), lambda b, g, i, lg: (b * nt + i, g)),
            scratch_shapes=[pltpu.VMEM((heads, hd, hd), F32)],
        ),
        out_shape=jax.ShapeDtypeStruct((m, n_heads * hd), BF16),
        compiler_params=_params("parallel", "parallel", "arbitrary"),
        name="retention",
    )(log_g, proj, proj, proj, proj, cos_t, sin_t)


def _att_bias_body(tab_ref, o_ref):
    qb, width = o_ref.shape[1], o_ref.shape[2]
    n = pl.cdiv(width + qb - 1, LANES) * LANES
    tab = jnp.broadcast_to(tab_ref[0], (SUBLANES, N_REL))
    u = lax.broadcasted_iota(jnp.int32, (N_REL, n), 1)
    kk = lax.broadcasted_iota(jnp.int32, (N_REL, n), 0)
    rel = jnp.clip(LEFT + qb - 1 - u, -(CHUNK - 1), REL_CLIP) + (CHUNK - 1)
    onehot = jnp.where(rel == kk, 1.0, 0.0).astype(BF16)
    e = jnp.zeros((SUBLANES, n), F32)
    rest = tab
    for _ in range(3):
        piece = rest.astype(BF16)
        e += jnp.dot(piece, onehot, preferred_element_type=F32)
        rest = rest - piece.astype(F32)
    eb = jnp.broadcast_to(e[0:1], (qb, n))
    bias = pltpu.roll(eb, n - (qb - 1), axis=1, stride=1, stride_axis=0)[:, :width]
    r = lax.broadcasted_iota(jnp.int32, (qb, width), 0) // CHUNK
    mc = lax.broadcasted_iota(jnp.int32, (qb, width), 1) // CHUNK
    in_band = (mc >= r) & (mc <= r + LEFT_CHUNKS)
    o_ref[0] = jnp.where(in_band, bias, MASKED)


def attention_bias(rel_table, qb):
    h = rel_table.shape[0]
    width = LEFT + qb
    return pl.pallas_call(
        _att_bias_body,
        grid=(h,),
        in_specs=[pl.BlockSpec((1, 1, N_REL), lambda i: (i, 0, 0))],
        out_specs=pl.BlockSpec((1, qb, width), lambda i: (i, 0, 0)),
        out_shape=jax.ShapeDtypeStruct((h, qb, width), F32),
        compiler_params=_params("parallel"),
        name="attention_bias",
    )(rel_table.reshape(h, 1, N_REL))


def _attention_body(heads, nkb, q_ref, *refs):
    k_refs, v_refs = refs[:nkb], refs[nkb:2 * nkb]
    bias_ref, o_ref = refs[2 * nkb], refs[2 * nkb + 1]
    step = pl.program_id(2)
    qb = q_ref.shape[0]
    scale = HEAD_DIM ** -0.5
    for hh in range(heads):
        lanes = slice(hh * HEAD_DIM, (hh + 1) * HEAD_DIM)
        q = q_ref[:, lanes]
        logits = []
        for j in range(nkb):
            s = lax.dot_general(q, k_refs[j][:, lanes], (((1,), (1,)), ((), ())),
                                preferred_element_type=F32)
            s = s * scale + bias_ref[hh, :, j * qb:(j + 1) * qb]
            if j < nkb - 1:
                s = jnp.where(step - (nkb - 1 - j) >= 0, s, MASKED)
            logits.append(s)
        mx = functools.reduce(jnp.maximum,
                              [jnp.max(s, axis=-1, keepdims=True) for s in logits])
        acc = jnp.zeros((qb, HEAD_DIM), F32)
        den = jnp.zeros((qb, 1), F32)
        for j in range(nkb):
            e = jnp.exp(logits[j] - mx)
            den += jnp.sum(e, axis=-1, keepdims=True)
            acc += jnp.dot(e.astype(BF16), v_refs[j][:, lanes], preferred_element_type=F32)
        o_ref[:, lanes] = (acc / den).astype(o_ref.dtype)


def band_attention(proj, bias, batch, t, n_heads, col0):
    m = batch * t
    qb = bias.shape[1]
    nkb = LEFT // qb + 1
    nq = t // qb
    hd = HEAD_DIM
    heads = _tile(n_heads, HEAD_GROUP)
    ng = n_heads // heads
    assert col0 % heads == 0
    cg0 = col0 // heads

    def kv_spec(which, j):
        back = nkb - 1 - j
        return pl.BlockSpec(
            (qb, heads * hd),
            lambda g, b, i: (b * nq + jnp.maximum(i - back, 0), cg0 + which * ng + g))

    in_specs = [pl.BlockSpec((qb, heads * hd), lambda g, b, i: (b * nq + i, cg0 + g))]
    in_specs += [kv_spec(1, j) for j in range(nkb)]
    in_specs += [kv_spec(2, j) for j in range(nkb)]
    in_specs += [pl.BlockSpec((heads, qb, LEFT + qb), lambda g, b, i: (g, 0, 0))]
    return pl.pallas_call(
        functools.partial(_attention_body, heads, nkb),
        grid=(ng, batch, nq),
        in_specs=in_specs,
        out_specs=pl.BlockSpec((qb, heads * hd), lambda g, b, i: (b * nq + i, g)),
        out_shape=jax.ShapeDtypeStruct((m, n_heads * hd), BF16),
        compiler_params=_params("parallel", "parallel", "arbitrary"),
        name="band_attention",
    )(*([proj] * (1 + 2 * nkb)), bias)


def _ffn_up_step(tiles_per_seq, nj, a_ref, ss_ref, wg_ref, wv_ref, cwg_ref, cwv_ref, cbg_ref,
                 cbv_ref, o_ref, cur_ref, prev_ref, halo_ref):
    tm = a_ref.shape[0]
    prev = jnp.maximum(pl.program_id(0) - 1, 0)
    jp = prev % nj
    seq_start = ((prev // nj) % tiles_per_seq) == 0

    above = [halo_ref[half, jp] for half in range(2)]
    for half in range(2):
        halo_ref[half, jp] = prev_ref[half, tm:, :]

    taps, biases = [], []
    for half, (w_ref, b_ref) in enumerate(((cwg_ref, cbg_ref), (cwv_ref, cbv_ref))):
        prev_ref[half, 0:SUBLANES, :] = jnp.where(seq_start, 0.0, above[half])
        w = w_ref[...]
        taps.append([w[k:k + 1] for k in range(CONV_WIDTH)])
        biases.append(b_ref[...])

    def conv(half, r0, rows):
        out = biases[half]
        for k in range(CONV_WIDTH):
            lo = SUBLANES + r0 - (CONV_WIDTH - 1 - k)
            out = out + taps[half][k] * prev_ref[half, lo:lo + rows, :]
        return out

    d = a_ref.shape[1]
    n_chunks = max(1, min(tm // FFN_EPILOGUE_ROWS, d // FFN_K_CHUNK))
    rows, kc = tm // n_chunks, d // n_chunks
    acc = [None, None]
    for c in range(n_chunks):
        a = a_ref[:, c * kc:(c + 1) * kc]
        for half, w_ref in enumerate((wg_ref, wv_ref)):
            w = w_ref[c * kc:(c + 1) * kc, :].astype(BF16)
            part = jnp.dot(a, w, preferred_element_type=F32)
            acc[half] = part if acc[half] is None else acc[half] + part
        r0 = c * rows
        g = conv(0, r0, rows)
        val = conv(1, r0, rows)
        o_ref[r0:r0 + rows, :] = (_silu(g) * val).astype(o_ref.dtype)
    scale = _row_scale(ss_ref, d)
    cur_ref[0, SUBLANES:, :] = acc[0] * scale
    cur_ref[1, SUBLANES:, :] = acc[1] * scale


def _ffn_up_body(tiles_per_seq, nj, *refs):
    io_refs, (raw_a, raw_b, halo_ref) = refs[:-3], refs[-3:]
    step = pl.program_id(0)
    run = functools.partial(_ffn_up_step, tiles_per_seq, nj, *io_refs)

    @pl.when(step == 0)
    def _():
        raw_b[...] = jnp.zeros_like(raw_b)
        halo_ref[...] = jnp.zeros_like(halo_ref)

    @pl.when(step % 2 == 0)
    def _():
        run(raw_a, raw_b, halo_ref)

    @pl.when(step % 2 == 1)
    def _():
        run(raw_b, raw_a, halo_ref)


def ffn_up(xg, ss, w, layer, conv_w, conv_b, t):
    m, d = xg.shape
    f = w.shape[2] // 2
    tm, tn = _tile(min(m, t), 1024), _tile(f, 2 * LANES)
    ni, nj = m // tm, f // tn
    n_tiles = ni * nj

    def cur(s):
        c = jnp.minimum(s, n_tiles - 1)
        return c // nj, c % nj

    def prev(s):
        p = jnp.maximum(s - 1, 0)
        return p // nj, p % nj

    def wspec(half):
        return pl.BlockSpec((None, d, tn), lambda s: (layer, 0, half * nj + cur(s)[1]))

    def cspec(rows, half):
        return pl.BlockSpec((rows, tn), lambda s: (0, half * nj + prev(s)[1]))

    raw = pltpu.VMEM((2, SUBLANES + tm, tn), F32)
    return pl.pallas_call(
        functools.partial(_ffn_up_body, t // tm, nj),
        grid=(n_tiles + 1,),
        in_specs=[pl.BlockSpec((tm, d), lambda s: (cur(s)[0], 0)),
                  _ss_in_spec(ss, tm, lambda s: cur(s)[0]), wspec(0), wspec(1),
                  cspec(CONV_WIDTH, 0), cspec(CONV_WIDTH, 1), cspec(1, 0), cspec(1, 1)],
        out_specs=pl.BlockSpec((tm, tn), lambda s: prev(s)),
        out_shape=jax.ShapeDtypeStruct((m, f), BF16),
        scratch_shapes=[raw, raw, pltpu.VMEM((2, nj, SUBLANES, tn), F32)],
        compiler_params=_params("arbitrary"),
        name="ffn_up",
    )(xg, ss, w, w, conv_w, conv_w, conv_b.reshape(1, 2 * f), conv_b.reshape(1, 2 * f))


def kernel(x, ln_mix, w_in, rel_bias, w_out, ln_ffn, w_up, conv_w, conv_b, w_down, ln_final):
    batch, t, d = x.shape
    depth = w_in.shape[0]
    m = batch * t
    n_ret = (d // 2) // HEAD_DIM
    n_att = rel_bias.shape[1]

    cos_t, sin_t = rope_tables(t)
    log_g = jnp.log(1.0 - 2.0 ** (-5.0 - jnp.arange(n_ret, dtype=F32)))
    ret_block = _tile(t, 256)
    att_block = _tile(LEFT, 256)

    h = x.reshape(m, d)
    xg, ss = prescale(h, ln_mix[0])
    for layer in range(depth):
        proj = in_proj(xg, ss, cast_weight(w_in, layer))
        ro = retention(proj, cos_t, sin_t, log_g, batch, t, n_ret, ret_block)
        bias = attention_bias(rel_bias[layer], att_block)
        ao = band_attention(proj, bias, batch, t, n_att, 4 * n_ret)
        h, xg, ss = out_proj_residual(ro, ao, cast_weight(w_out, layer), h, ln_ffn[layer])

        act = ffn_up(xg, ss, w_up, layer, conv_w[layer], conv_b[layer], t)
        w_dn = cast_weight(w_down, layer)
        if layer + 1 < depth:
            h, xg, ss = down_proj_residual(act, w_dn, h, ln_mix[layer + 1])
        else:
            h = down_proj_residual(act, w_dn, h)
    return rmsnorm(h, ln_final, F32).reshape(batch, t, d)
```

```python
import functools

import jax
import jax.numpy as jnp
from jax import lax
from jax.experimental import pallas as pl
from jax.experimental.pallas import tpu as pltpu

CHUNK = 64
HEAD_DIM = 128
LEFT_CHUNKS = 8
LEFT = LEFT_CHUNKS * CHUNK
REL_CLIP = 128
N_REL = REL_CLIP + CHUNK
CONV_WIDTH = 3
ROPE_BASE = 10000.0
EPS = 1e-6

LANES = 128
SUBLANES = 8
VMEM_LIMIT = 56 * 1024 * 1024
HEAD_GROUP = 8
FFN_EPILOGUE_ROWS = 64
FFN_K_CHUNK = 256
MASKED = -1e30
LOG2_E = 1.4426950408889634

F32 = jnp.float32
BF16 = jnp.bfloat16


def _params(*sem):
    return pltpu.CompilerParams(dimension_semantics=sem, vmem_limit_bytes=VMEM_LIMIT)


def _tile(n, want):
    t = min(n, want)
    while n % t:
        t //= 2
    return t


def _rmsnorm_body(x_ref, g_ref, o_ref):
    x = x_ref[...]
    ms = jnp.mean(x * x, axis=-1, keepdims=True)
    o_ref[...] = (x * lax.rsqrt(ms + EPS) * g_ref[...]).astype(o_ref.dtype)


def rmsnorm(x, gain, out_dtype):
    m, d = x.shape
    tm = _tile(m, 256)
    return pl.pallas_call(
        _rmsnorm_body,
        grid=(m // tm,),
        in_specs=[pl.BlockSpec((tm, d), lambda i: (i, 0)),
                  pl.BlockSpec((1, d), lambda i: (0, 0))],
        out_specs=pl.BlockSpec((tm, d), lambda i: (i, 0)),
        out_shape=jax.ShapeDtypeStruct((m, d), out_dtype),
        compiler_params=_params("parallel"),
        name="rmsnorm",
    )(x, gain.reshape(1, d))


def _sum_squares(x):
    return jnp.sum(x * x, axis=-1, keepdims=True)


def _row_scale(ss_ref, d):
    return lax.rsqrt(jnp.sum(ss_ref[...], axis=0) / d + EPS)


def _ss_in_spec(ss, tm, row_of):
    return pl.BlockSpec((ss.shape[0], tm, 1), lambda *idx: (0, row_of(*idx), 0))


def _prescale_body(x_ref, g_ref, o_ref, ss_ref):
    x = x_ref[...]
    o_ref[...] = (x * g_ref[...]).astype(o_ref.dtype)
    ss_ref[...] = _sum_squares(x)


def prescale(x, gain):
    m, d = x.shape
    tm = _tile(m, 256)
    return pl.pallas_call(
        _prescale_body,
        grid=(m // tm,),
        in_specs=[pl.BlockSpec((tm, d), lambda i: (i, 0)),
                  pl.BlockSpec((1, d), lambda i: (0, 0))],
        out_specs=[pl.BlockSpec((tm, d), lambda i: (i, 0)),
                   pl.BlockSpec((None, tm, 1), lambda i: (0, i, 0))],
        out_shape=[jax.ShapeDtypeStruct((m, d), BF16),
                   jax.ShapeDtypeStruct((1, m, 1), F32)],
        compiler_params=_params("parallel"),
        name="prescale",
    )(x, gain.reshape(1, d))


def _in_proj_body(a_ref, ss_ref, b_ref, o_ref):
    acc = jnp.dot(a_ref[...], b_ref[...], preferred_element_type=F32)
    o_ref[...] = (acc * _row_scale(ss_ref, a_ref.shape[1])).astype(o_ref.dtype)


def in_proj(a, ss, b):
    m, k = a.shape
    n = b.shape[1]
    tm, tn = _tile(m, 1024), _tile(n, 1024)
    return pl.pallas_call(
        _in_proj_body,
        grid=(m // tm, n // tn),
        in_specs=[pl.BlockSpec((tm, k), lambda i, j: (i, 0)),
                  _ss_in_spec(ss, tm, lambda i, j: i),
                  pl.BlockSpec((k, tn), lambda i, j: (0, j))],
        out_specs=pl.BlockSpec((tm, tn), lambda i, j: (i, j)),
        out_shape=jax.ShapeDtypeStruct((m, n), BF16),
        compiler_params=_params("parallel", "arbitrary"),
        name="in_proj",
    )(a, ss, b)


def _emit_residual(h, g_ref, o_ref, hg_ref, ss_ref):
    o_ref[...] = h
    hg_ref[...] = (h * g_ref[...]).astype(hg_ref.dtype)
    ss_ref[...] = _sum_squares(h)


def _residual_out(m, n, tm, tn, idx):
    specs = [pl.BlockSpec((tm, tn), lambda *g: idx(*g)),
             pl.BlockSpec((tm, tn), lambda *g: idx(*g)),
             pl.BlockSpec((None, tm, 1), lambda *g: (idx(*g)[1], idx(*g)[0], 0))]
    shapes = [jax.ShapeDtypeStruct((m, n), F32), jax.ShapeDtypeStruct((m, n), BF16),
              jax.ShapeDtypeStruct((n // tn, m, 1), F32)]
    return specs, shapes


def _out_proj_body(a1_ref, a2_ref, b1_ref, b2_ref, r_ref, g_ref, o_ref, hg_ref, ss_ref):
    acc = jnp.dot(a1_ref[...], b1_ref[...], preferred_element_type=F32)
    acc += jnp.dot(a2_ref[...], b2_ref[...], preferred_element_type=F32)
    _emit_residual(r_ref[...] + acc, g_ref, o_ref, hg_ref, ss_ref)


def out_proj_residual(a1, a2, w, res, next_gain):
    m, k1 = a1.shape
    k2 = a2.shape[1]
    n = w.shape[1]
    assert k1 == k2 and w.shape[0] == k1 + k2
    tm, tn = _tile(m, 1024), _tile(n, 512)
    out_specs, out_shape = _residual_out(m, n, tm, tn, lambda i, j: (i, j))
    return pl.pallas_call(
        _out_proj_body,
        grid=(m // tm, n // tn),
        in_specs=[pl.BlockSpec((tm, k1), lambda i, j: (i, 0)),
                  pl.BlockSpec((tm, k2), lambda i, j: (i, 0)),
                  pl.BlockSpec((k1, tn), lambda i, j: (0, j)),
                  pl.BlockSpec((k2, tn), lambda i, j: (1, j)),
                  pl.BlockSpec((tm, tn), lambda i, j: (i, j)),
                  pl.BlockSpec((1, tn), lambda i, j: (0, j))],
        out_specs=out_specs,
        out_shape=out_shape,
        compiler_params=_params("parallel", "arbitrary"),
        name="out_proj",
    )(a1, a2, w, w, res, next_gain.reshape(1, n))


def _down_proj_body(nk, k_last, emit_next, a_ref, b_ref, r_ref, *refs):
    k = pl.program_id(2)
    acc_ref = refs[-1]

    @pl.when(k == 0)
    def _():
        acc_ref[...] = jnp.dot(a_ref[...], b_ref[...], preferred_element_type=F32)

    @pl.when((k > 0) & (k < nk - 1))
    def _():
        acc_ref[...] += jnp.dot(a_ref[...], b_ref[...], preferred_element_type=F32)

    @pl.when(k == nk - 1)
    def _():
        tail = jnp.dot(a_ref[:, :k_last], b_ref[:k_last, :], preferred_element_type=F32)
        h = r_ref[...] + (acc_ref[...] + tail)
        if emit_next:
            _emit_residual(h, *refs[:4])
        else:
            refs[0][...] = h


def down_proj_residual(a, w, res, next_gain=None):
    m, k = a.shape
    n = w.shape[1]
    tm, tn = _tile(m, 1024), _tile(n, 1024)
    tk = -(-k // (4 * 2 * LANES)) * 2 * LANES
    nk = -(-k // tk)
    assert nk >= 2
    k_last = k - (nk - 1) * tk
    emit_next = next_gain is not None
    in_specs = [pl.BlockSpec((tm, tk), lambda i, j, kk: (i, kk)),
                pl.BlockSpec((tk, tn), lambda i, j, kk: (kk, j)),
                pl.BlockSpec((tm, tn), lambda i, j, kk: (i, j))]
    args = [a, w, res]
    out_specs, out_shape = _residual_out(m, n, tm, tn, lambda i, j, kk: (i, j))
    if emit_next:
        in_specs.append(pl.BlockSpec((1, tn), lambda i, j, kk: (0, j)))
        args.append(next_gain.reshape(1, n))
    else:
        out_specs, out_shape = out_specs[0], out_shape[0]
    return pl.pallas_call(
        functools.partial(_down_proj_body, nk, k_last, emit_next),
        grid=(m // tm, n // tn, nk),
        in_specs=in_specs,
        out_specs=out_specs,
        out_shape=out_shape,
        scratch_shapes=[pltpu.VMEM((tm, tn), F32)],
        compiler_params=_params("parallel", "arbitrary", "arbitrary"),
        name="down_proj",
    )(*args)


def _cast_body(x_ref, o_ref):
    o_ref[...] = x_ref[...].astype(o_ref.dtype)


def cast_weight(w, layer):
    _, k, n = w.shape
    tk, tn = _tile(k, 1024), _tile(n, 2048)
    return pl.pallas_call(
        _cast_body,
        grid=(k // tk, n // tn),
        in_specs=[pl.BlockSpec((None, tk, tn), lambda i, j: (layer, i, j))],
        out_specs=pl.BlockSpec((tk, tn), lambda i, j: (i, j)),
        out_shape=jax.ShapeDtypeStruct((k, n), BF16),
        compiler_params=_params("parallel", "parallel"),
        name="cast_weight",
    )(w)


def _rope_body(inv_ref, cos_ref, sin_ref):
    tt = cos_ref.shape[0]
    row = lax.broadcasted_iota(jnp.int32, (tt, LANES), 0) + pl.program_id(0) * tt
    lane = lax.broadcasted_iota(jnp.int32, (tt, LANES), 1)
    ang = row.astype(F32) * inv_ref[...]
    cos_ref[...] = jnp.cos(ang)
    sin_ref[...] = jnp.where(lane < HEAD_DIM // 2, -1.0, 1.0) * jnp.sin(ang)


def rope_tables(t):
    half = HEAD_DIM // 2
    inv = 1.0 / (ROPE_BASE ** jnp.linspace(0.0, 1.0, half, dtype=F32))
    inv2 = jnp.concatenate([inv, inv]).reshape(1, HEAD_DIM)
    tt = _tile(t, 512)
    return pl.pallas_call(
        _rope_body,
        grid=(t // tt,),
        in_specs=[pl.BlockSpec((1, HEAD_DIM), lambda i: (0, 0))],
        out_specs=[pl.BlockSpec((tt, HEAD_DIM), lambda i: (i, 0)),
                   pl.BlockSpec((tt, HEAD_DIM), lambda i: (i, 0))],
        out_shape=[jax.ShapeDtypeStruct((t, HEAD_DIM), F32)] * 2,
        compiler_params=_params("parallel"),
        name="rope_tables",
    )(inv2)


def _silu(x):
    return x * jax.nn.sigmoid(x)


def _staggered(n_items, stages):
    carried = [None] * n_items
    for t in range(n_items + len(stages) - 1):
        for s, stage in enumerate(stages):
            item = t - s
            if 0 <= item < n_items:
                carried[item] = stage(item, carried[item])


def _retention_body(heads, logg_ref, q_ref, k_ref, v_ref, g_ref, cos_ref, sin_ref,
                    o_ref, state_ref, intra_ref, qdec_ref, kdec_ref):
    group = pl.program_id(1)
    step = pl.program_id(2)
    c = q_ref.shape[0]

    @pl.when(step == 0)
    def _():
        state_ref[...] = jnp.zeros_like(state_ref)
        row = lax.broadcasted_iota(jnp.int32, (c, c), 0)
        col = lax.broadcasted_iota(jnp.int32, (c, c), 1)
        diff = (row - col).astype(F32)
        pos = lax.broadcasted_iota(jnp.int32, (c, HEAD_DIM), 0).astype(F32)
        for hh in range(heads):
            lg = logg_ref[group * heads + hh]
            intra_ref[hh] = jnp.where(diff >= 0, jnp.exp(jnp.maximum(diff, 0.0) * lg), 0.0)
            qdec_ref[hh] = jnp.exp((pos + 1.0) * lg)
            kdec_ref[hh] = jnp.exp((c - 1.0 - pos) * lg)

    cos = cos_ref[...]
    sin = sin_ref[...]

    def rot(x):
        return x * cos + pltpu.roll(x, HEAD_DIM // 2, axis=1) * sin

    def lanes(hh):
        return slice(hh * HEAD_DIM, (hh + 1) * HEAD_DIM)

    def rotate(hh, _):
        q = rot(q_ref[:, lanes(hh)].astype(F32))
        k = rot(k_ref[:, lanes(hh)].astype(F32)) * (HEAD_DIM ** -0.5)
        return q.astype(BF16), k.astype(BF16), (k * kdec_ref[hh]).astype(BF16)

    def scores(hh, qk):
        qb, kb, kd = qk
        s = lax.dot_general(qb, kb, (((1,), (1,)), ((), ())), preferred_element_type=F32)
        return qb, kd, (s * intra_ref[hh]).astype(BF16)

    def outputs(hh, qks):
        qb, kd, s = qks
        v = v_ref[:, lanes(hh)]
        state = state_ref[hh]
        o = jnp.dot(s, v, preferred_element_type=F32)
        o += jnp.dot(qb, state.astype(BF16), preferred_element_type=F32) * qdec_ref[hh]
        kv = lax.dot_general(kd, v, (((0,), (0,)), ((), ())), preferred_element_type=F32)
        block_decay = jnp.exp(jnp.full((1, HEAD_DIM), float(c), F32)
                              * logg_ref[group * heads + hh])
        state_ref[hh] = state * block_decay + kv
        return o

    def norm_gate(hh, o):
        o = o * lax.rsqrt(jnp.mean(o * o, axis=-1, keepdims=True) + EPS)
        o_ref[:, lanes(hh)] = (_silu(g_ref[:, lanes(hh)].astype(F32)) * o).astype(o_ref.dtype)

    _staggered(heads, (rotate, scores, outputs, norm_gate))


def retention(proj, cos_t, sin_t, log_g, batch, t, n_heads, block):
    m = batch * t
    nt = t // block
    hd = HEAD_DIM
    heads = _tile(n_heads, HEAD_GROUP)
    ng = n_heads // heads

    def col(which):
        return pl.BlockSpec((block, heads * hd),
                            lambda b, g, i, lg: (b * nt + i, which * ng + g))

    tab = pl.BlockSpec((block, hd), lambda b, g, i, lg: (i, 0))
    return pl.pallas_call(
        functools.partial(_retention_body, heads),
        grid_spec=pltpu.PrefetchScalarGridSpec(
            num_scalar_prefetch=1,
            grid=(batch, ng, nt),
            in_specs=[col(0), col(1), col(2), col(3), tab, tab],
            out_specs=pl.BlockSpec((block, heads * hd), lambda b, g, i, lg: (b * nt + i, g)),
            scratch_shapes=[pltpu.VMEM((heads, hd, hd), F32),
                            pltpu.VMEM((heads, block, block), F32),
                            pltpu.VMEM((heads, block, hd), F32),
                            pltpu.VMEM((heads, block, hd), F32)],
        ),
        out_shape=jax.ShapeDtypeStruct((m, n_heads * hd), BF16),
        compiler_params=_params("parallel", "parallel", "arbitrary"),
        name="retention",
    )(log_g, proj, proj, proj, proj, cos_t, sin_t)


def _att_bias_body(tab_ref, o_ref):
    qb, width = o_ref.shape
    n = pl.cdiv(width + qb - 1, LANES) * LANES
    tab = jnp.broadcast_to(tab_ref[0], (SUBLANES, N_REL))
    u = lax.broadcasted_iota(jnp.int32, (N_REL, n), 1)
    kk = lax.broadcasted_iota(jnp.int32, (N_REL, n), 0)
    rel = jnp.clip(LEFT + qb - 1 - u, -(CHUNK - 1), REL_CLIP) + (CHUNK - 1)
    onehot = jnp.where(rel == kk, 1.0, 0.0).astype(BF16)
    e = jnp.zeros((SUBLANES, n), F32)
    rest = tab
    for _ in range(3):
        piece = rest.astype(BF16)
        e += jnp.dot(piece, onehot, preferred_element_type=F32)
        rest = rest - piece.astype(F32)
    eb = jnp.broadcast_to(e[0:1], (qb, n))
    bias = pltpu.roll(eb, n - (qb - 1), axis=1, stride=1, stride_axis=0)[:, :width]
    r = lax.broadcasted_iota(jnp.int32, (qb, width), 0) // CHUNK
    mc = lax.broadcasted_iota(jnp.int32, (qb, width), 1) // CHUNK
    blocks_before_start = (width // qb - 1) - pl.program_id(0)
    visible = (mc >= r) & (mc <= r + LEFT_CHUNKS) & (mc * CHUNK >= blocks_before_start * qb)
    o_ref[...] = jnp.where(visible, bias * (HEAD_DIM ** 0.5), MASKED)


def attention_bias(rel_table, qb):
    h = rel_table.shape[0]
    width = LEFT + qb
    nkb = width // qb
    return pl.pallas_call(
        _att_bias_body,
        grid=(nkb, h),
        in_specs=[pl.BlockSpec((1, 1, N_REL), lambda v, i: (i, 0, 0))],
        out_specs=pl.BlockSpec((None, None, qb, width), lambda v, i: (v, i, 0, 0)),
        out_shape=jax.ShapeDtypeStruct((nkb, h, qb, width), F32),
        compiler_params=_params("parallel", "parallel"),
        name="attention_bias",
    )(rel_table.reshape(h, 1, N_REL))


def _attention_body(heads, nkb, q_ref, *refs):
    k_refs, v_refs = refs[:nkb], refs[nkb:2 * nkb]
    bias_ref, o_ref = refs[2 * nkb], refs[2 * nkb + 1]
    qb = q_ref.shape[0]
    exp2_scale = (HEAD_DIM ** -0.5) * LOG2_E

    def lanes(hh):
        return slice(hh * HEAD_DIM, (hh + 1) * HEAD_DIM)

    def logits(hh, _):
        q = q_ref[:, lanes(hh)]
        return [lax.dot_general(q, k_refs[j][:, lanes(hh)], (((1,), (1,)), ((), ())),
                                preferred_element_type=F32)
                + bias_ref[hh, :, j * qb:(j + 1) * qb] for j in range(nkb)]

    def weights(hh, raw):
        mx = functools.reduce(jnp.maximum, [jnp.max(s, axis=-1, keepdims=True) for s in raw])
        es = [jnp.exp2((s - mx) * exp2_scale) for s in raw]
        den = functools.reduce(jnp.add, [jnp.sum(e, axis=-1, keepdims=True) for e in es])
        return [e.astype(BF16) for e in es], den

    def values(hh, es_den):
        es, den = es_den
        acc = functools.reduce(jnp.add, [
            jnp.dot(es[j], v_refs[j][:, lanes(hh)], preferred_element_type=F32)
            for j in range(nkb)])
        o_ref[:, lanes(hh)] = (acc / den).astype(o_ref.dtype)

    _staggered(heads, (logits, weights, values))


def band_attention(proj, bias, batch, t, n_heads, col0):
    m = batch * t
    nkb, _, qb, _ = bias.shape
    nq = t // qb
    hd = HEAD_DIM
    heads = _tile(n_heads, HEAD_GROUP)
    ng = n_heads // heads
    assert col0 % heads == 0
    cg0 = col0 // heads

    def kv_spec(which, j):
        back = nkb - 1 - j
        return pl.BlockSpec(
            (qb, heads * hd),
            lambda g, b, i: (b * nq + jnp.maximum(i - back, 0), cg0 + which * ng + g))

    in_specs = [pl.BlockSpec((qb, heads * hd), lambda g, b, i: (b * nq + i, cg0 + g))]
    in_specs += [kv_spec(1, j) for j in range(nkb)]
    in_specs += [kv_spec(2, j) for j in range(nkb)]
    in_specs += [pl.BlockSpec((None, heads, qb, LEFT + qb),
                              lambda g, b, i: (jnp.minimum(i, nkb - 1), g, 0, 0))]
    return pl.pallas_call(
        functools.partial(_attention_body, heads, nkb),
        grid=(ng, batch, nq),
        in_specs=in_specs,
        out_specs=pl.BlockSpec((qb, heads * hd), lambda g, b, i: (b * nq + i, g)),
        out_shape=jax.ShapeDtypeStruct((m, n_heads * hd), BF16),
        compiler_params=_params("parallel", "parallel", "arbitrary"),
        name="band_attention",
    )(*([proj] * (1 + 2 * nkb)), bias)


def _ffn_up_step(tiles_per_seq, nj, a_ref, ss_ref, wg_ref, wv_ref, cwg_ref, cwv_ref, cbg_ref,
                 cbv_ref, o_ref, cur_ref, prev_ref, halo_ref):
    tm = a_ref.shape[0]
    prev = jnp.maximum(pl.program_id(0) - 1, 0)
    jp = prev % nj
    seq_start = ((prev // nj) % tiles_per_seq) == 0

    above = [halo_ref[half, jp] for half in range(2)]
    for half in range(2):
        halo_ref[half, jp] = prev_ref[half, tm:, :]

    taps, biases = [], []
    for half, (w_ref, b_ref) in enumerate(((cwg_ref, cbg_ref), (cwv_ref, cbv_ref))):
        prev_ref[half, 0:SUBLANES, :] = jnp.where(seq_start, 0.0, above[half])
        w = w_ref[...]
        taps.append([w[k:k + 1] for k in range(CONV_WIDTH)])
        biases.append(b_ref[...])

    def conv(half, r0, rows):
        out = biases[half]
        for k in range(CONV_WIDTH):
            lo = SUBLANES + r0 - (CONV_WIDTH - 1 - k)
            out = out + taps[half][k] * prev_ref[half, lo:lo + rows, :]
        return out

    d = a_ref.shape[1]
    n_chunks = max(1, min(tm // FFN_EPILOGUE_ROWS, d // FFN_K_CHUNK))
    rows, kc = tm // n_chunks, d // n_chunks
    acc = [None, None]
    for c in range(n_chunks):
        a = a_ref[:, c * kc:(c + 1) * kc]
        for half, w_ref in enumerate((wg_ref, wv_ref)):
            w = w_ref[c * kc:(c + 1) * kc, :].astype(BF16)
            part = jnp.dot(a, w, preferred_element_type=F32)
            acc[half] = part if acc[half] is None else acc[half] + part
        r0 = c * rows
        g = conv(0, r0, rows)
        val = conv(1, r0, rows)
        o_ref[r0:r0 + rows, :] = (_silu(g) * val).astype(o_ref.dtype)
    scale = _row_scale(ss_ref, d)
    cur_ref[0, SUBLANES:, :] = acc[0] * scale
    cur_ref[1, SUBLANES:, :] = acc[1] * scale


def _ffn_up_body(tiles_per_seq, nj, *refs):
    io_refs, (raw_a, raw_b, halo_ref) = refs[:-3], refs[-3:]
    step = pl.program_id(0)
    run = functools.partial(_ffn_up_step, tiles_per_seq, nj, *io_refs)

    @pl.when(step == 0)
    def _():
        raw_b[...] = jnp.zeros_like(raw_b)
        halo_ref[...] = jnp.zeros_like(halo_ref)

    @pl.when(step % 2 == 0)
    def _():
        run(raw_a, raw_b, halo_ref)

    @pl.when(step % 2 == 1)
    def _():
        run(raw_b, raw_a, halo_ref)


def ffn_up(xg, ss, w, layer, conv_w, conv_b, t):
    m, d = xg.shape
    f = w.shape[2] // 2
    tm, tn = _tile(min(m, t), 1024), _tile(f, 2 * LANES)
    ni, nj = m // tm, f // tn
    n_tiles = ni * nj

    def cur(s):
        c = jnp.minimum(s, n_tiles - 1)
        return c // nj, c % nj

    def prev(s):
        p = jnp.maximum(s - 1, 0)
        return p // nj, p % nj

    def wspec(half):
        return pl.BlockSpec((None, d, tn), lambda s: (layer, 0, half * nj + cur(s)[1]))

    def cspec(rows, half):
        return pl.BlockSpec((rows, tn), lambda s: (0, half * nj + prev(s)[1]))

    raw = pltpu.VMEM((2, SUBLANES + tm, tn), F32)
    return pl.pallas_call(
        functools.partial(_ffn_up_body, t // tm, nj),
        grid=(n_tiles + 1,),
        in_specs=[pl.BlockSpec((tm, d), lambda s: (cur(s)[0], 0)),
                  _ss_in_spec(ss, tm, lambda s: cur(s)[0]), wspec(0), wspec(1),
                  cspec(CONV_WIDTH, 0), cspec(CONV_WIDTH, 1), cspec(1, 0), cspec(1, 1)],
        out_specs=pl.BlockSpec((tm, tn), lambda s: prev(s)),
        out_shape=jax.ShapeDtypeStruct((m, f), BF16),
        scratch_shapes=[raw, raw, pltpu.VMEM((2, nj, SUBLANES, tn), F32)],
        compiler_params=_params("arbitrary"),
        name="ffn_up",
    )(xg, ss, w, w, conv_w, conv_w, conv_b.reshape(1, 2 * f), conv_b.reshape(1, 2 * f))


def kernel(x, ln_mix, w_in, rel_bias, w_out, ln_ffn, w_up, conv_w, conv_b, w_down, ln_final):
    batch, t, d = x.shape
    depth = w_in.shape[0]
    m = batch * t
    n_ret = (d // 2) // HEAD_DIM
    n_att = rel_bias.shape[1]

    cos_t, sin_t = rope_tables(t)
    log_g = jnp.log(1.0 - 2.0 ** (-5.0 - jnp.arange(n_ret, dtype=F32)))
    ret_block = _tile(t, 256)
    att_block = _tile(LEFT, 256)

    h = x.reshape(m, d)
    xg, ss = prescale(h, ln_mix[0])
    for layer in range(depth):
        proj = in_proj(xg, ss, cast_weight(w_in, layer))
        ro = retention(proj, cos_t, sin_t, log_g, batch, t, n_ret, ret_block)
        bias = attention_bias(rel_bias[layer], att_block)
        ao = band_attention(proj, bias, batch, t, n_att, 4 * n_ret)
        h, xg, ss = out_proj_residual(ro, ao, cast_weight(w_out, layer), h, ln_ffn[layer])

        act = ffn_up(xg, ss, w_up, layer, conv_w[layer], conv_b[layer], t)
        w_dn = cast_weight(w_down, layer)
        if layer + 1 < depth:
            h, xg, ss = down_proj_residual(act, w_dn, h, ln_mix[layer + 1])
        else:
            h = down_proj_residual(act, w_dn, h)
    return rmsnorm(h, ln_final, F32).reshape(batch, t, d)
```

```python
import functools

import jax
import jax.numpy as jnp
from jax import lax
from jax.experimental import pallas as pl
from jax.experimental.pallas import tpu as pltpu

CHUNK = 64
HEAD_DIM = 128
LEFT_CHUNKS = 8
LEFT = LEFT_CHUNKS * CHUNK
REL_CLIP = 128
N_REL = REL_CLIP + CHUNK
CONV_WIDTH = 3
ROPE_BASE = 10000.0
EPS = 1e-6

LANES = 128
SUBLANES = 8
VMEM_LIMIT = 56 * 1024 * 1024
HEAD_GROUP = 8
FFN_EPILOGUE_ROWS = 64
FFN_K_CHUNK = 256
MASKED = -1e30
LOG2_E = 1.4426950408889634

F32 = jnp.float32
BF16 = jnp.bfloat16


def _params(*sem):
    return pltpu.CompilerParams(dimension_semantics=sem, vmem_limit_bytes=VMEM_LIMIT)


def _tile(n, want):
    t = min(n, want)
    while n % t:
        t //= 2
    return t


def _rmsnorm_body(x_ref, g_ref, o_ref):
    x = x_ref[...]
    ms = jnp.mean(x * x, axis=-1, keepdims=True)
    o_ref[...] = (x * lax.rsqrt(ms + EPS) * g_ref[...]).astype(o_ref.dtype)


def rmsnorm(x, gain, out_dtype):
    m, d = x.shape
    tm = _tile(m, 256)
    return pl.pallas_call(
        _rmsnorm_body,
        grid=(m // tm,),
        in_specs=[pl.BlockSpec((tm, d), lambda i: (i, 0)),
                  pl.BlockSpec((1, d), lambda i: (0, 0))],
        out_specs=pl.BlockSpec((tm, d), lambda i: (i, 0)),
        out_shape=jax.ShapeDtypeStruct((m, d), out_dtype),
        compiler_params=_params("parallel"),
        name="rmsnorm",
    )(x, gain.reshape(1, d))


def _sum_squares(x):
    return jnp.sum(x * x, axis=-1, keepdims=True)


def _row_scale(ss_ref, d):
    return lax.rsqrt(jnp.sum(ss_ref[...], axis=0) / d + EPS)


def _ss_in_spec(ss, tm, row_of):
    return pl.BlockSpec((ss.shape[0], tm, 1), lambda *idx: (0, row_of(*idx), 0))


def _prescale_body(x_ref, g_ref, o_ref, ss_ref):
    x = x_ref[...]
    o_ref[...] = (x * g_ref[...]).astype(o_ref.dtype)
    ss_ref[...] = _sum_squares(x)


def prescale(x, gain):
    m, d = x.shape
    tm = _tile(m, 256)
    return pl.pallas_call(
        _prescale_body,
        grid=(m // tm,),
        in_specs=[pl.BlockSpec((tm, d), lambda i: (i, 0)),
                  pl.BlockSpec((1, d), lambda i: (0, 0))],
        out_specs=[pl.BlockSpec((tm, d), lambda i: (i, 0)),
                   pl.BlockSpec((None, tm, 1), lambda i: (0, i, 0))],
        out_shape=[jax.ShapeDtypeStruct((m, d), BF16),
                   jax.ShapeDtypeStruct((1, m, 1), F32)],
        compiler_params=_params("parallel"),
        name="prescale",
    )(x, gain.reshape(1, d))


def _in_proj_body(a_ref, ss_ref, b_ref, o_ref):
    acc = jnp.dot(a_ref[...], b_ref[...], preferred_element_type=F32)
    o_ref[...] = (acc * _row_scale(ss_ref, a_ref.shape[1])).astype(o_ref.dtype)


def in_proj(a, ss, b):
    m, k = a.shape
    n = b.shape[1]
    tm, tn = _tile(m, 1024), _tile(n, 1024)
    return pl.pallas_call(
        _in_proj_body,
        grid=(m // tm, n // tn),
        in_specs=[pl.BlockSpec((tm, k), lambda i, j: (i, 0)),
                  _ss_in_spec(ss, tm, lambda i, j: i),
                  pl.BlockSpec((k, tn), lambda i, j: (0, j))],
        out_specs=pl.BlockSpec((tm, tn), lambda i, j: (i, j)),
        out_shape=jax.ShapeDtypeStruct((m, n), BF16),
        compiler_params=_params("parallel", "arbitrary"),
        name="in_proj",
    )(a, ss, b)


def _emit_residual(h, col_tile, g_ref, o_ref, hg_ref, ss_ref):
    o_ref[...] = h
    hg_ref[...] = (h * g_ref[...]).astype(hg_ref.dtype)
    part = _sum_squares(h)

    @pl.when(col_tile == 0)
    def _():
        ss_ref[...] = part

    @pl.when(col_tile > 0)
    def _():
        ss_ref[...] += part


def _residual_out(m, n, tm, tn, idx):
    specs = [pl.BlockSpec((tm, tn), lambda *g: idx(*g)),
             pl.BlockSpec((tm, tn), lambda *g: idx(*g)),
             pl.BlockSpec((None, tm, 1), lambda *g: (0, idx(*g)[0], 0))]
    shapes = [jax.ShapeDtypeStruct((m, n), F32), jax.ShapeDtypeStruct((m, n), BF16),
              jax.ShapeDtypeStruct((1, m, 1), F32)]
    return specs, shapes


def _out_proj_body(a1_ref, a2_ref, b1_ref, b2_ref, r_ref, g_ref, o_ref, hg_ref, ss_ref):
    acc = jnp.dot(a1_ref[...], b1_ref[...], preferred_element_type=F32)
    acc += jnp.dot(a2_ref[...], b2_ref[...], preferred_element_type=F32)
    _emit_residual(r_ref[...] + acc, pl.program_id(1), g_ref, o_ref, hg_ref, ss_ref)


def out_proj_residual(a1, a2, w, res, next_gain):
    m, k1 = a1.shape
    k2 = a2.shape[1]
    n = w.shape[1]
    assert k1 == k2 and w.shape[0] == k1 + k2
    tm, tn = _tile(m, 1024), _tile(n, 512)
    out_specs, out_shape = _residual_out(m, n, tm, tn, lambda i, j: (i, j))
    return pl.pallas_call(
        _out_proj_body,
        grid=(m // tm, n // tn),
        in_specs=[pl.BlockSpec((tm, k1), lambda i, j: (i, 0)),
                  pl.BlockSpec((tm, k2), lambda i, j: (i, 0)),
                  pl.BlockSpec((k1, tn), lambda i, j: (0, j)),
                  pl.BlockSpec((k2, tn), lambda i, j: (1, j)),
                  pl.BlockSpec((tm, tn), lambda i, j: (i, j)),
                  pl.BlockSpec((1, tn), lambda i, j: (0, j))],
        out_specs=out_specs,
        out_shape=out_shape,
        compiler_params=_params("parallel", "arbitrary"),
        name="out_proj",
    )(a1, a2, w, w, res, next_gain.reshape(1, n))


def _down_proj_body(nk, k_last, emit_next, a_ref, b_ref, r_ref, *refs):
    col_tile = pl.program_id(1)
    k = pl.program_id(2)
    acc_ref = refs[-1]

    @pl.when(k == 0)
    def _():
        acc_ref[...] = jnp.dot(a_ref[...], b_ref[...], preferred_element_type=F32)

    @pl.when((k > 0) & (k < nk - 1))
    def _():
        acc_ref[...] += jnp.dot(a_ref[...], b_ref[...], preferred_element_type=F32)

    @pl.when(k == nk - 1)
    def _():
        tail = jnp.dot(a_ref[:, :k_last], b_ref[:k_last, :], preferred_element_type=F32)
        h = r_ref[...] + (acc_ref[...] + tail)
        if emit_next:
            _emit_residual(h, col_tile, *refs[:4])
        else:
            refs[0][...] = h


def down_proj_residual(a, w, res, next_gain=None):
    m, k = a.shape
    n = w.shape[1]
    tm, tn = _tile(m, 1024), _tile(n, 1024)
    tk = -(-k // (4 * 2 * LANES)) * 2 * LANES
    nk = -(-k // tk)
    assert nk >= 2
    k_last = k - (nk - 1) * tk
    emit_next = next_gain is not None
    in_specs = [pl.BlockSpec((tm, tk), lambda i, j, kk: (i, kk)),
                pl.BlockSpec((tk, tn), lambda i, j, kk: (kk, j)),
                pl.BlockSpec((tm, tn), lambda i, j, kk: (i, j))]
    args = [a, w, res]
    out_specs, out_shape = _residual_out(m, n, tm, tn, lambda i, j, kk: (i, j))
    if emit_next:
        in_specs.append(pl.BlockSpec((1, tn), lambda i, j, kk: (0, j)))
        args.append(next_gain.reshape(1, n))
    else:
        out_specs, out_shape = out_specs[0], out_shape[0]
    return pl.pallas_call(
        functools.partial(_down_proj_body, nk, k_last, emit_next),
        grid=(m // tm, n // tn, nk),
        in_specs=in_specs,
        out_specs=out_specs,
        out_shape=out_shape,
        scratch_shapes=[pltpu.VMEM((tm, tn), F32)],
        compiler_params=_params("parallel", "arbitrary", "arbitrary"),
        name="down_proj",
    )(*args)


def _cast_body(x_ref, o_ref):
    o_ref[...] = x_ref[...].astype(o_ref.dtype)


def cast_weight(w, layer):
    _, k, n = w.shape
    tk, tn = _tile(k, 1024), _tile(n, 2048)
    return pl.pallas_call(
        _cast_body,
        grid=(k // tk, n // tn),
        in_specs=[pl.BlockSpec((None, tk, tn), lambda i, j: (layer, i, j))],
        out_specs=pl.BlockSpec((tk, tn), lambda i, j: (i, j)),
        out_shape=jax.ShapeDtypeStruct((k, n), BF16),
        compiler_params=_params("parallel", "parallel"),
        name="cast_weight",
    )(w)


def _rope_body(inv_ref, cos_ref, sin_ref):
    tt = cos_ref.shape[0]
    row = lax.broadcasted_iota(jnp.int32, (tt, LANES), 0) + pl.program_id(0) * tt
    lane = lax.broadcasted_iota(jnp.int32, (tt, LANES), 1)
    ang = row.astype(F32) * inv_ref[...]
    cos_ref[...] = jnp.cos(ang)
    sin_ref[...] = jnp.where(lane < HEAD_DIM // 2, -1.0, 1.0) * jnp.sin(ang)


def rope_tables(t):
    half = HEAD_DIM // 2
    inv = 1.0 / (ROPE_BASE ** jnp.linspace(0.0, 1.0, half, dtype=F32))
    inv2 = jnp.concatenate([inv, inv]).reshape(1, HEAD_DIM)
    tt = _tile(t, 512)
    return pl.pallas_call(
        _rope_body,
        grid=(t // tt,),
        in_specs=[pl.BlockSpec((1, HEAD_DIM), lambda i: (0, 0))],
        out_specs=[pl.BlockSpec((tt, HEAD_DIM), lambda i: (i, 0)),
                   pl.BlockSpec((tt, HEAD_DIM), lambda i: (i, 0))],
        out_shape=[jax.ShapeDtypeStruct((t, HEAD_DIM), F32)] * 2,
        compiler_params=_params("parallel"),
        name="rope_tables",
    )(inv2)


def _silu(x):
    return x * jax.nn.sigmoid(x)


def _staggered(n_items, stages):
    carried = [None] * n_items
    for t in range(n_items + len(stages) - 1):
        for s, stage in enumerate(stages):
            item = t - s
            if 0 <= item < n_items:
                carried[item] = stage(item, carried[item])


def _retention_body(heads, logg_ref, q_ref, k_ref, v_ref, g_ref, cos_ref, sin_ref,
                    o_ref, state_ref, intra_ref, qdec_ref, kdec_ref):
    group = pl.program_id(1)
    step = pl.program_id(2)
    c = q_ref.shape[0]

    @pl.when(step == 0)
    def _():
        state_ref[...] = jnp.zeros_like(state_ref)
        row = lax.broadcasted_iota(jnp.int32, (c, c), 0)
        col = lax.broadcasted_iota(jnp.int32, (c, c), 1)
        diff = (row - col).astype(F32)
        pos = lax.broadcasted_iota(jnp.int32, (c, HEAD_DIM), 0).astype(F32)
        for hh in range(heads):
            lg = logg_ref[group * heads + hh]
            intra_ref[hh] = jnp.where(diff >= 0, jnp.exp(jnp.maximum(diff, 0.0) * lg), 0.0)
            qdec_ref[hh] = jnp.exp((pos + 1.0) * lg)
            kdec_ref[hh] = jnp.exp((c - 1.0 - pos) * lg)

    cos = cos_ref[...]
    sin = sin_ref[...]

    def rot(x):
        return x * cos + pltpu.roll(x, HEAD_DIM // 2, axis=1) * sin

    def lanes(hh):
        return slice(hh * HEAD_DIM, (hh + 1) * HEAD_DIM)

    def rotate(hh, _):
        q = rot(q_ref[:, lanes(hh)].astype(F32))
        k = rot(k_ref[:, lanes(hh)].astype(F32)) * (HEAD_DIM ** -0.5)
        return q.astype(BF16), k.astype(BF16), (k * kdec_ref[hh]).astype(BF16)

    def scores(hh, qk):
        qb, kb, kd = qk
        s = lax.dot_general(qb, kb, (((1,), (1,)), ((), ())), preferred_element_type=F32)
        return qb, kd, (s * intra_ref[hh]).astype(BF16)

    def outputs(hh, qks):
        qb, kd, s = qks
        v = v_ref[:, lanes(hh)]
        state = state_ref[hh]
        o = jnp.dot(s, v, preferred_element_type=F32)
        o += jnp.dot(qb, state.astype(BF16), preferred_element_type=F32) * qdec_ref[hh]
        kv = lax.dot_general(kd, v, (((0,), (0,)), ((), ())), preferred_element_type=F32)
        block_decay = jnp.exp(jnp.full((1, HEAD_DIM), float(c), F32)
                              * logg_ref[group * heads + hh])
        state_ref[hh] = state * block_decay + kv
        return o

    def norm_gate(hh, o):
        o = o * lax.rsqrt(jnp.mean(o * o, axis=-1, keepdims=True) + EPS)
        o_ref[:, lanes(hh)] = (_silu(g_ref[:, lanes(hh)].astype(F32)) * o).astype(o_ref.dtype)

    _staggered(heads, (rotate, scores, outputs, norm_gate))


def retention(proj, cos_t, sin_t, log_g, batch, t, n_heads, block):
    m = batch * t
    nt = t // block
    hd = HEAD_DIM
    heads = _tile(n_heads, HEAD_GROUP)
    ng = n_heads // heads

    def col(which):
        return pl.BlockSpec((block, heads * hd),
                            lambda b, g, i, lg: (b * nt + i, which * ng + g))

    tab = pl.BlockSpec((block, hd), lambda b, g, i, lg: (i, 0))
    return pl.pallas_call(
        functools.partial(_retention_body, heads),
        grid_spec=pltpu.PrefetchScalarGridSpec(
            num_scalar_prefetch=1,
            grid=(batch, ng, nt),
            in_specs=[col(0), col(1), col(2), col(3), tab, tab],
            out_specs=pl.BlockSpec((block, heads * hd), lambda b, g, i, lg: (b * nt + i, g)),
            scratch_shapes=[pltpu.VMEM((heads, hd, hd), F32),
                            pltpu.VMEM((heads, block, block), F32),
                            pltpu.VMEM((heads, block, hd), F32),
                            pltpu.VMEM((heads, block, hd), F32)],
        ),
        out_shape=jax.ShapeDtypeStruct((m, n_heads * hd), BF16),
        compiler_params=_params("parallel", "parallel", "arbitrary"),
        name="retention",
    )(log_g, proj, proj, proj, proj, cos_t, sin_t)


def _att_bias_body(tab_ref, o_ref):
    qb, width = o_ref.shape
    n = pl.cdiv(width + qb - 1, LANES) * LANES
    tab = jnp.broadcast_to(tab_ref[0], (SUBLANES, N_REL))
    u = lax.broadcasted_iota(jnp.int32, (N_REL, n), 1)
    kk = lax.broadcasted_iota(jnp.int32, (N_REL, n), 0)
    rel = jnp.clip(LEFT + qb - 1 - u, -(CHUNK - 1), REL_CLIP) + (CHUNK - 1)
    onehot = jnp.where(rel == kk, 1.0, 0.0).astype(BF16)
    e = jnp.zeros((SUBLANES, n), F32)
    rest = tab
    for _ in range(3):
        piece = rest.astype(BF16)
        e += jnp.dot(piece, onehot, preferred_element_type=F32)
        rest = rest - piece.astype(F32)
    eb = jnp.broadcast_to(e[0:1], (qb, n))
    bias = pltpu.roll(eb, n - (qb - 1), axis=1, stride=1, stride_axis=0)[:, :width]
    r = lax.broadcasted_iota(jnp.int32, (qb, width), 0) // CHUNK
    mc = lax.broadcasted_iota(jnp.int32, (qb, width), 1) // CHUNK
    blocks_before_start = (width // qb - 1) - pl.program_id(0)
    visible = (mc >= r) & (mc <= r + LEFT_CHUNKS) & (mc * CHUNK >= blocks_before_start * qb)
    o_ref[...] = jnp.where(visible, bias * (HEAD_DIM ** 0.5), MASKED)


def attention_bias(rel_table, qb):
    h = rel_table.shape[0]
    width = LEFT + qb
    nkb = width // qb
    return pl.pallas_call(
        _att_bias_body,
        grid=(nkb, h),
        in_specs=[pl.BlockSpec((1, 1, N_REL), lambda v, i: (i, 0, 0))],
        out_specs=pl.BlockSpec((None, None, qb, width), lambda v, i: (v, i, 0, 0)),
        out_shape=jax.ShapeDtypeStruct((nkb, h, qb, width), F32),
        compiler_params=_params("parallel", "parallel"),
        name="attention_bias",
    )(rel_table.reshape(h, 1, N_REL))


def _attention_body(heads, nkb, q_ref, *refs):
    k_refs, v_refs = refs[:nkb], refs[nkb:2 * nkb]
    bias_ref, o_ref = refs[2 * nkb], refs[2 * nkb + 1]
    qb = q_ref.shape[0]
    exp2_scale = (HEAD_DIM ** -0.5) * LOG2_E

    def lanes(hh):
        return slice(hh * HEAD_DIM, (hh + 1) * HEAD_DIM)

    def logits(hh, _):
        q = q_ref[:, lanes(hh)]
        return [lax.dot_general(q, k_refs[j][:, lanes(hh)], (((1,), (1,)), ((), ())),
                                preferred_element_type=F32)
                + bias_ref[hh, :, j * qb:(j + 1) * qb] for j in range(nkb)]

    def weights(hh, raw):
        mx = jnp.max(functools.reduce(jnp.maximum, raw), axis=-1, keepdims=True)
        es = [jnp.exp2((s - mx) * exp2_scale) for s in raw]
        den = jnp.sum(functools.reduce(jnp.add, es), axis=-1, keepdims=True)
        return [e.astype(BF16) for e in es], den

    def values(hh, es_den):
        es, den = es_den
        acc = functools.reduce(jnp.add, [
            jnp.dot(es[j], v_refs[j][:, lanes(hh)], preferred_element_type=F32)
            for j in range(nkb)])
        o_ref[:, lanes(hh)] = (acc / den).astype(o_ref.dtype)

    _staggered(heads, (logits, weights, values))


def band_attention(proj, bias, batch, t, n_heads, col0):
    m = batch * t
    nkb, _, qb, _ = bias.shape
    nq = t // qb
    hd = HEAD_DIM
    heads = _tile(n_heads, HEAD_GROUP)
    ng = n_heads // heads
    assert col0 % heads == 0
    cg0 = col0 // heads

    def kv_spec(which, j):
        back = nkb - 1 - j
        return pl.BlockSpec(
            (qb, heads * hd),
            lambda g, b, i: (b * nq + jnp.maximum(i - back, 0), cg0 + which * ng + g))

    in_specs = [pl.BlockSpec((qb, heads * hd), lambda g, b, i: (b * nq + i, cg0 + g))]
    in_specs += [kv_spec(1, j) for j in range(nkb)]
    in_specs += [kv_spec(2, j) for j in range(nkb)]
    in_specs += [pl.BlockSpec((None, heads, qb, LEFT + qb),
                              lambda g, b, i: (jnp.minimum(i, nkb - 1), g, 0, 0))]
    return pl.pallas_call(
        functools.partial(_attention_body, heads, nkb),
        grid=(ng, batch, nq),
        in_specs=in_specs,
        out_specs=pl.BlockSpec((qb, heads * hd), lambda g, b, i: (b * nq + i, g)),
        out_shape=jax.ShapeDtypeStruct((m, n_heads * hd), BF16),
        compiler_params=_params("parallel", "parallel", "arbitrary"),
        name="band_attention",
    )(*([proj] * (1 + 2 * nkb)), bias)


def _ffn_up_step(tiles_per_seq, nj, a_ref, ss_ref, wg_ref, wv_ref, cwg_ref, cwv_ref, cbg_ref,
                 cbv_ref, o_ref, cur_ref, prev_ref, halo_ref):
    tm = a_ref.shape[0]
    prev = jnp.maximum(pl.program_id(0) - 1, 0)
    jp = prev % nj
    seq_start = ((prev // nj) % tiles_per_seq) == 0

    above = [halo_ref[half, jp] for half in range(2)]
    for half in range(2):
        halo_ref[half, jp] = prev_ref[half, tm:, :]

    taps, biases = [], []
    for half, (w_ref, b_ref) in enumerate(((cwg_ref, cbg_ref), (cwv_ref, cbv_ref))):
        prev_ref[half, 0:SUBLANES, :] = jnp.where(seq_start, 0.0, above[half])
        w = w_ref[...]
        taps.append([w[k:k + 1] for k in range(CONV_WIDTH)])
        biases.append(b_ref[...])

    def conv(half, r0, rows):
        out = biases[half]
        for k in range(CONV_WIDTH):
            lo = SUBLANES + r0 - (CONV_WIDTH - 1 - k)
            out = out + taps[half][k] * prev_ref[half, lo:lo + rows, :]
        return out

    d = a_ref.shape[1]
    n_chunks = max(1, min(tm // FFN_EPILOGUE_ROWS, d // FFN_K_CHUNK))
    rows, kc = tm // n_chunks, d // n_chunks
    acc = [None, None]
    for c in range(n_chunks):
        a = a_ref[:, c * kc:(c + 1) * kc]
        for half, w_ref in enumerate((wg_ref, wv_ref)):
            w = w_ref[c * kc:(c + 1) * kc, :].astype(BF16)
            part = jnp.dot(a, w, preferred_element_type=F32)
            acc[half] = part if acc[half] is None else acc[half] + part
        r0 = c * rows
        g = conv(0, r0, rows)
        val = conv(1, r0, rows)
        o_ref[r0:r0 + rows, :] = (_silu(g) * val).astype(o_ref.dtype)
    scale = _row_scale(ss_ref, d)
    cur_ref[0, SUBLANES:, :] = acc[0] * scale
    cur_ref[1, SUBLANES:, :] = acc[1] * scale


def _ffn_up_body(tiles_per_seq, nj, *refs):
    io_refs, (raw_a, raw_b, halo_ref) = refs[:-3], refs[-3:]
    step = pl.program_id(0)
    run = functools.partial(_ffn_up_step, tiles_per_seq, nj, *io_refs)

    @pl.when(step == 0)
    def _():
        raw_b[...] = jnp.zeros_like(raw_b)
        halo_ref[...] = jnp.zeros_like(halo_ref)

    @pl.when(step % 2 == 0)
    def _():
        run(raw_a, raw_b, halo_ref)

    @pl.when(step % 2 == 1)
    def _():
        run(raw_b, raw_a, halo_ref)


def ffn_up(xg, ss, w, layer, conv_w, conv_b, t):
    m, d = xg.shape
    f = w.shape[2] // 2
    tm, tn = _tile(min(m, t), 1024), _tile(f, 2 * LANES)
    ni, nj = m // tm, f // tn
    n_tiles = ni * nj

    def cur(s):
        c = jnp.minimum(s, n_tiles - 1)
        return c // nj, c % nj

    def prev(s):
        p = jnp.maximum(s - 1, 0)
        return p // nj, p % nj

    def wspec(half):
        return pl.BlockSpec((None, d, tn), lambda s: (layer, 0, half * nj + cur(s)[1]))

    def cspec(rows, half):
        return pl.BlockSpec((rows, tn), lambda s: (0, half * nj + prev(s)[1]))

    raw = pltpu.VMEM((2, SUBLANES + tm, tn), F32)
    return pl.pallas_call(
        functools.partial(_ffn_up_body, t // tm, nj),
        grid=(n_tiles + 1,),
        in_specs=[pl.BlockSpec((tm, d), lambda s: (cur(s)[0], 0)),
                  _ss_in_spec(ss, tm, lambda s: cur(s)[0]), wspec(0), wspec(1),
                  cspec(CONV_WIDTH, 0), cspec(CONV_WIDTH, 1), cspec(1, 0), cspec(1, 1)],
        out_specs=pl.BlockSpec((tm, tn), lambda s: prev(s)),
        out_shape=jax.ShapeDtypeStruct((m, f), BF16),
        scratch_shapes=[raw, raw, pltpu.VMEM((2, nj, SUBLANES, tn), F32)],
        compiler_params=_params("arbitrary"),
        name="ffn_up",
    )(xg, ss, w, w, conv_w, conv_w, conv_b.reshape(1, 2 * f), conv_b.reshape(1, 2 * f))


def kernel(x, ln_mix, w_in, rel_bias, w_out, ln_ffn, w_up, conv_w, conv_b, w_down, ln_final):
    batch, t, d = x.shape
    depth = w_in.shape[0]
    m = batch * t
    n_ret = (d // 2) // HEAD_DIM
    n_att = rel_bias.shape[1]

    cos_t, sin_t = rope_tables(t)
    log_g = jnp.log(1.0 - 2.0 ** (-5.0 - jnp.arange(n_ret, dtype=F32)))
    ret_block = _tile(t, 256)
    att_block = _tile(LEFT, 256)

    h = x.reshape(m, d)
    xg, ss = prescale(h, ln_mix[0])
    for layer in range(depth):
        proj = in_proj(xg, ss, cast_weight(w_in, layer))
        ro = retention(proj, cos_t, sin_t, log_g, batch, t, n_ret, ret_block)
        bias = attention_bias(rel_bias[layer], att_block)
        ao = band_attention(proj, bias, batch, t, n_att, 4 * n_ret)
        h, xg, ss = out_proj_residual(ro, ao, cast_weight(w_out, layer), h, ln_ffn[layer])

        act = ffn_up(xg, ss, w_up, layer, conv_w[layer], conv_b[layer], t)
        w_dn = cast_weight(w_down, layer)
        if layer + 1 < depth:
            h, xg, ss = down_proj_residual(act, w_dn, h, ln_mix[layer + 1])
        else:
            h = down_proj_residual(act, w_dn, h)
    return rmsnorm(h, ln_final, F32).reshape(batch, t, d)
```

```python
import functools

import jax
import jax.numpy as jnp
from jax import lax
from jax.experimental import pallas as pl
from jax.experimental.pallas import tpu as pltpu

CHUNK = 64
HEAD_DIM = 128
LEFT_CHUNKS = 8
LEFT = LEFT_CHUNKS * CHUNK
REL_CLIP = 128
N_REL = REL_CLIP + CHUNK
CONV_WIDTH = 3
ROPE_BASE = 10000.0
EPS = 1e-6

LANES = 128
SUBLANES = 8
VMEM_LIMIT = 56 * 1024 * 1024
RETENTION_HEADS = 8
ATTENTION_HEADS = 8
FFN_EPILOGUE_ROWS = 64
FFN_K_CHUNK = 256
MASKED = -1e30
LOG2_E = 1.4426950408889634

F32 = jnp.float32
BF16 = jnp.bfloat16


def _params(*sem):
    return pltpu.CompilerParams(dimension_semantics=sem, vmem_limit_bytes=VMEM_LIMIT)


def _tile(n, want):
    t = min(n, want)
    while n % t:
        t //= 2
    return t


def _rmsnorm_body(x_ref, g_ref, o_ref):
    x = x_ref[...]
    ms = jnp.mean(x * x, axis=-1, keepdims=True)
    o_ref[...] = (x * lax.rsqrt(ms + EPS) * g_ref[...]).astype(o_ref.dtype)


def rmsnorm(x, gain, out_dtype):
    m, d = x.shape
    tm = _tile(m, 256)
    return pl.pallas_call(
        _rmsnorm_body,
        grid=(m // tm,),
        in_specs=[pl.BlockSpec((tm, d), lambda i: (i, 0)),
                  pl.BlockSpec((1, d), lambda i: (0, 0))],
        out_specs=pl.BlockSpec((tm, d), lambda i: (i, 0)),
        out_shape=jax.ShapeDtypeStruct((m, d), out_dtype),
        compiler_params=_params("parallel"),
        name="rmsnorm",
    )(x, gain.reshape(1, d))


def _sum_squares(x):
    return jnp.sum(x * x, axis=-1, keepdims=True)


def _row_scale(ss_ref, d):
    return lax.rsqrt(jnp.sum(ss_ref[...], axis=0) / d + EPS)


def _ss_in_spec(ss, tm, row_of):
    return pl.BlockSpec((ss.shape[0], tm, 1), lambda *idx: (0, row_of(*idx), 0))


def _prescale_body(x_ref, g_ref, o_ref, ss_ref):
    x = x_ref[...]
    o_ref[...] = (x * g_ref[...]).astype(o_ref.dtype)
    ss_ref[...] = _sum_squares(x)


def prescale(x, gain):
    m, d = x.shape
    tm = _tile(m, 256)
    return pl.pallas_call(
        _prescale_body,
        grid=(m // tm,),
        in_specs=[pl.BlockSpec((tm, d), lambda i: (i, 0)),
                  pl.BlockSpec((1, d), lambda i: (0, 0))],
        out_specs=[pl.BlockSpec((tm, d), lambda i: (i, 0)),
                   pl.BlockSpec((None, tm, 1), lambda i: (0, i, 0))],
        out_shape=[jax.ShapeDtypeStruct((m, d), BF16),
                   jax.ShapeDtypeStruct((1, m, 1), F32)],
        compiler_params=_params("parallel"),
        name="prescale",
    )(x, gain.reshape(1, d))


def _in_proj_body(a_ref, ss_ref, b_ref, o_ref):
    acc = jnp.dot(a_ref[...], b_ref[...], preferred_element_type=F32)
    o_ref[...] = (acc * _row_scale(ss_ref, a_ref.shape[1])).astype(o_ref.dtype)


def in_proj(a, ss, b):
    m, k = a.shape
    n = b.shape[1]
    tm, tn = _tile(m, 1024), _tile(n, 1024)
    return pl.pallas_call(
        _in_proj_body,
        grid=(m // tm, n // tn),
        in_specs=[pl.BlockSpec((tm, k), lambda i, j: (i, 0)),
                  _ss_in_spec(ss, tm, lambda i, j: i),
                  pl.BlockSpec((k, tn), lambda i, j: (0, j))],
        out_specs=pl.BlockSpec((tm, tn), lambda i, j: (i, j)),
        out_shape=jax.ShapeDtypeStruct((m, n), BF16),
        compiler_params=_params("parallel", "arbitrary"),
        name="in_proj",
    )(a, ss, b)


def _emit_residual(h, col_tile, g_ref, o_ref, hg_ref, ss_ref):
    o_ref[...] = h
    hg_ref[...] = (h * g_ref[...]).astype(hg_ref.dtype)
    part = _sum_squares(h)

    @pl.when(col_tile == 0)
    def _():
        ss_ref[...] = part

    @pl.when(col_tile > 0)
    def _():
        ss_ref[...] += part


def _residual_out(m, n, tm, tn, idx):
    specs = [pl.BlockSpec((tm, tn), lambda *g: idx(*g)),
             pl.BlockSpec((tm, tn), lambda *g: idx(*g)),
             pl.BlockSpec((None, tm, 1), lambda *g: (0, idx(*g)[0], 0))]
    shapes = [jax.ShapeDtypeStruct((m, n), F32), jax.ShapeDtypeStruct((m, n), BF16),
              jax.ShapeDtypeStruct((1, m, 1), F32)]
    return specs, shapes


def _out_proj_body(a1_ref, a2_ref, b1_ref, b2_ref, r_ref, g_ref, o_ref, hg_ref, ss_ref):
    acc = jnp.dot(a1_ref[...], b1_ref[...], preferred_element_type=F32)
    acc += jnp.dot(a2_ref[...], b2_ref[...], preferred_element_type=F32)
    _emit_residual(r_ref[...] + acc, pl.program_id(1), g_ref, o_ref, hg_ref, ss_ref)


def out_proj_residual(a1, a2, w, res, next_gain):
    m, k1 = a1.shape
    k2 = a2.shape[1]
    n = w.shape[1]
    assert k1 == k2 and w.shape[0] == k1 + k2
    tm, tn = _tile(m, 1024), _tile(n, 512)
    out_specs, out_shape = _residual_out(m, n, tm, tn, lambda i, j: (i, j))
    return pl.pallas_call(
        _out_proj_body,
        grid=(m // tm, n // tn),
        in_specs=[pl.BlockSpec((tm, k1), lambda i, j: (i, 0)),
                  pl.BlockSpec((tm, k2), lambda i, j: (i, 0)),
                  pl.BlockSpec((k1, tn), lambda i, j: (0, j)),
                  pl.BlockSpec((k2, tn), lambda i, j: (1, j)),
                  pl.BlockSpec((tm, tn), lambda i, j: (i, j)),
                  pl.BlockSpec((1, tn), lambda i, j: (0, j))],
        out_specs=out_specs,
        out_shape=out_shape,
        compiler_params=_params("parallel", "arbitrary"),
        name="out_proj",
    )(a1, a2, w, w, res, next_gain.reshape(1, n))


def _down_proj_body(nk, k_last, emit_next, a_ref, b_ref, r_ref, *refs):
    col_tile = pl.program_id(1)
    k = pl.program_id(2)
    acc_ref = refs[-1]

    @pl.when(k == 0)
    def _():
        acc_ref[...] = jnp.dot(a_ref[...], b_ref[...], preferred_element_type=F32)

    @pl.when((k > 0) & (k < nk - 1))
    def _():
        acc_ref[...] += jnp.dot(a_ref[...], b_ref[...], preferred_element_type=F32)

    @pl.when(k == nk - 1)
    def _():
        tail = jnp.dot(a_ref[:, :k_last], b_ref[:k_last, :], preferred_element_type=F32)
        h = r_ref[...] + (acc_ref[...] + tail)
        if emit_next:
            _emit_residual(h, col_tile, *refs[:4])
        else:
            refs[0][...] = h


def down_proj_residual(a, w, res, next_gain=None):
    m, k = a.shape
    n = w.shape[1]
    tm, tn = _tile(m, 1024), _tile(n, 1024)
    tk = -(-k // (4 * 2 * LANES)) * 2 * LANES
    nk = -(-k // tk)
    assert nk >= 2
    k_last = k - (nk - 1) * tk
    emit_next = next_gain is not None
    in_specs = [pl.BlockSpec((tm, tk), lambda i, j, kk: (i, kk)),
                pl.BlockSpec((tk, tn), lambda i, j, kk: (kk, j)),
                pl.BlockSpec((tm, tn), lambda i, j, kk: (i, j))]
    args = [a, w, res]
    out_specs, out_shape = _residual_out(m, n, tm, tn, lambda i, j, kk: (i, j))
    if emit_next:
        in_specs.append(pl.BlockSpec((1, tn), lambda i, j, kk: (0, j)))
        args.append(next_gain.reshape(1, n))
    else:
        out_specs, out_shape = out_specs[0], out_shape[0]
    return pl.pallas_call(
        functools.partial(_down_proj_body, nk, k_last, emit_next),
        grid=(m // tm, n // tn, nk),
        in_specs=in_specs,
        out_specs=out_specs,
        out_shape=out_shape,
        scratch_shapes=[pltpu.VMEM((tm, tn), F32)],
        compiler_params=_params("parallel", "arbitrary", "arbitrary"),
        name="down_proj",
    )(*args)


def _cast_body(x_ref, o_ref):
    o_ref[...] = x_ref[...].astype(o_ref.dtype)


def cast_weight(w, layer):
    _, k, n = w.shape
    tk, tn = _tile(k, 1024), _tile(n, 2048)
    return pl.pallas_call(
        _cast_body,
        grid=(k // tk, n // tn),
        in_specs=[pl.BlockSpec((None, tk, tn), lambda i, j: (layer, i, j))],
        out_specs=pl.BlockSpec((tk, tn), lambda i, j: (i, j)),
        out_shape=jax.ShapeDtypeStruct((k, n), BF16),
        compiler_params=_params("parallel", "parallel"),
        name="cast_weight",
    )(w)


def _cast_stream(w, layer, n_steps, step_of):
    _, k, n = w.shape
    rows = next(r for r in (2 * SUBLANES << p for p in range(16))
                if k % r == 0 and k // r <= n_steps)
    last = k // rows - 1

    def block(*g):
        return jnp.minimum(step_of(*g), last)

    return (pl.BlockSpec((None, rows, n), lambda *g: (layer, block(*g), 0)),
            pl.BlockSpec((rows, n), lambda *g: (block(*g), 0)),
            jax.ShapeDtypeStruct((k, n), BF16))


def _cast_streams(casts, n_steps, step_of):
    specs = [_cast_stream(w, layer, n_steps, step_of) for w, layer in casts]
    return ([s[0] for s in specs], [s[1] for s in specs], [s[2] for s in specs],
            [w for w, _ in casts])


def _rope_body(inv_ref, cos_ref, sin_ref):
    tt = cos_ref.shape[0]
    row = lax.broadcasted_iota(jnp.int32, (tt, LANES), 0) + pl.program_id(0) * tt
    lane = lax.broadcasted_iota(jnp.int32, (tt, LANES), 1)
    ang = row.astype(F32) * inv_ref[...]
    cos_ref[...] = jnp.cos(ang)
    sin_ref[...] = jnp.where(lane < HEAD_DIM // 2, -1.0, 1.0) * jnp.sin(ang)


def rope_tables(t):
    half = HEAD_DIM // 2
    inv = 1.0 / (ROPE_BASE ** jnp.linspace(0.0, 1.0, half, dtype=F32))
    inv2 = jnp.concatenate([inv, inv]).reshape(1, HEAD_DIM)
    tt = _tile(t, 512)
    return pl.pallas_call(
        _rope_body,
        grid=(t // tt,),
        in_specs=[pl.BlockSpec((1, HEAD_DIM), lambda i: (0, 0))],
        out_specs=[pl.BlockSpec((tt, HEAD_DIM), lambda i: (i, 0)),
                   pl.BlockSpec((tt, HEAD_DIM), lambda i: (i, 0))],
        out_shape=[jax.ShapeDtypeStruct((t, HEAD_DIM), F32)] * 2,
        compiler_params=_params("parallel"),
        name="rope_tables",
    )(inv2)


def _silu(x):
    return x * jax.nn.sigmoid(x)


def _staggered(n_items, stages):
    carried = [None] * n_items
    for t in range(n_items + len(stages) - 1):
        for s, stage in enumerate(stages):
            item = t - s
            if 0 <= item < n_items:
                carried[item] = stage(item, carried[item])


def _retention_body(heads, n_casts, logg_ref, q_ref, k_ref, v_ref, g_ref, cos_ref, sin_ref,
                    *refs):
    cast_in, o_ref, cast_out = refs[:n_casts], refs[n_casts], refs[n_casts + 1:2 * n_casts + 1]
    state_ref, intra_ref, qdec_ref, kdec_ref = refs[2 * n_casts + 1:]
    group = pl.program_id(1)
    step = pl.program_id(2)
    c = q_ref.shape[0]

    for src, dst in zip(cast_in, cast_out):
        dst[...] = src[...].astype(dst.dtype)

    @pl.when(step == 0)
    def _():
        state_ref[...] = jnp.zeros_like(state_ref)
        row = lax.broadcasted_iota(jnp.int32, (c, c), 0)
        col = lax.broadcasted_iota(jnp.int32, (c, c), 1)
        diff = (row - col).astype(F32)
        pos = lax.broadcasted_iota(jnp.int32, (c, HEAD_DIM), 0).astype(F32)
        for hh in range(heads):
            lg = logg_ref[group * heads + hh]
            intra_ref[hh] = jnp.where(diff >= 0, jnp.exp(jnp.maximum(diff, 0.0) * lg), 0.0)
            qdec_ref[hh] = jnp.exp((pos + 1.0) * lg)
            kdec_ref[hh] = jnp.exp((c - 1.0 - pos) * lg)

    cos = cos_ref[...]
    sin = sin_ref[...]

    def rot(x):
        return x * cos + pltpu.roll(x, HEAD_DIM // 2, axis=1) * sin

    def lanes(hh):
        return slice(hh * HEAD_DIM, (hh + 1) * HEAD_DIM)

    def rotate(hh, _):
        q = rot(q_ref[:, lanes(hh)].astype(F32))
        k = rot(k_ref[:, lanes(hh)].astype(F32)) * (HEAD_DIM ** -0.5)
        return q.astype(BF16), k.astype(BF16), (k * kdec_ref[hh]).astype(BF16)

    def scores(hh, qk):
        qb, kb, kd = qk
        s = lax.dot_general(qb, kb, (((1,), (1,)), ((), ())), preferred_element_type=F32)
        return qb, kd, (s * intra_ref[hh]).astype(BF16)

    def outputs(hh, qks):
        qb, kd, s = qks
        v = v_ref[:, lanes(hh)]
        state = state_ref[hh]
        o = jnp.dot(s, v, preferred_element_type=F32)
        o += jnp.dot(qb, state.astype(BF16), preferred_element_type=F32) * qdec_ref[hh]
        kv = lax.dot_general(kd, v, (((0,), (0,)), ((), ())), preferred_element_type=F32)
        block_decay = jnp.exp(jnp.full((1, HEAD_DIM), float(c), F32)
                              * logg_ref[group * heads + hh])
        state_ref[hh] = state * block_decay + kv
        return o

    def norm_gate(hh, o):
        o = o * lax.rsqrt(jnp.mean(o * o, axis=-1, keepdims=True) + EPS)
        o_ref[:, lanes(hh)] = (_silu(g_ref[:, lanes(hh)].astype(F32)) * o).astype(o_ref.dtype)

    _staggered(heads, (rotate, scores, outputs, norm_gate))


def retention(proj, cos_t, sin_t, log_g, batch, t, n_heads, block, casts=()):
    m = batch * t
    nt = t // block
    hd = HEAD_DIM
    heads = _tile(n_heads, RETENTION_HEADS)
    ng = n_heads // heads
    cast_in, cast_out, cast_shapes, cast_args = _cast_streams(
        casts, batch * ng * nt, lambda b, g, i, lg: (b * ng + g) * nt + i)

    def col(which):
        return pl.BlockSpec((block, heads * hd),
                            lambda b, g, i, lg: (b * nt + i, which * ng + g))

    tab = pl.BlockSpec((block, hd), lambda b, g, i, lg: (i, 0))
    return pl.pallas_call(
        functools.partial(_retention_body, heads, len(casts)),
        grid_spec=pltpu.PrefetchScalarGridSpec(
            num_scalar_prefetch=1,
            grid=(batch, ng, nt),
            in_specs=[col(0), col(1), col(2), col(3), tab, tab] + cast_in,
            out_specs=[pl.BlockSpec((block, heads * hd),
                                    lambda b, g, i, lg: (b * nt + i, g))] + cast_out,
            scratch_shapes=[pltpu.VMEM((heads, hd, hd), F32),
                            pltpu.VMEM((heads, block, block), F32),
                            pltpu.VMEM((heads, block, hd), F32),
                            pltpu.VMEM((heads, block, hd), F32)],
        ),
        out_shape=[jax.ShapeDtypeStruct((m, n_heads * hd), BF16)] + cast_shapes,
        compiler_params=_params("arbitrary", "arbitrary", "arbitrary"),
        name="retention",
    )(log_g, proj, proj, proj, proj, cos_t, sin_t, *cast_args)


def _att_bias_body(tab_ref, o_ref):
    qb, width = o_ref.shape
    n = pl.cdiv(width + qb - 1, LANES) * LANES
    tab = jnp.broadcast_to(tab_ref[0], (SUBLANES, N_REL))
    u = lax.broadcasted_iota(jnp.int32, (N_REL, n), 1)
    kk = lax.broadcasted_iota(jnp.int32, (N_REL, n), 0)
    rel = jnp.clip(LEFT + qb - 1 - u, -(CHUNK - 1), REL_CLIP) + (CHUNK - 1)
    onehot = jnp.where(rel == kk, 1.0, 0.0).astype(BF16)
    e = jnp.zeros((SUBLANES, n), F32)
    rest = tab
    for _ in range(3):
        piece = rest.astype(BF16)
        e += jnp.dot(piece, onehot, preferred_element_type=F32)
        rest = rest - piece.astype(F32)
    eb = jnp.broadcast_to(e[0:1], (qb, n))
    bias = pltpu.roll(eb, n - (qb - 1), axis=1, stride=1, stride_axis=0)[:, :width]
    r = lax.broadcasted_iota(jnp.int32, (qb, width), 0) // CHUNK
    mc = lax.broadcasted_iota(jnp.int32, (qb, width), 1) // CHUNK
    blocks_before_start = (width // qb - 1) - pl.program_id(0)
    visible = (mc >= r) & (mc <= r + LEFT_CHUNKS) & (mc * CHUNK >= blocks_before_start * qb)
    o_ref[...] = jnp.where(visible, bias * (HEAD_DIM ** 0.5), MASKED)


def attention_bias(rel_table, qb):
    h = rel_table.shape[0]
    width = LEFT + qb
    nkb = width // qb
    return pl.pallas_call(
        _att_bias_body,
        grid=(nkb, h),
        in_specs=[pl.BlockSpec((1, 1, N_REL), lambda v, i: (i, 0, 0))],
        out_specs=pl.BlockSpec((None, None, qb, width), lambda v, i: (v, i, 0, 0)),
        out_shape=jax.ShapeDtypeStruct((nkb, h, qb, width), F32),
        compiler_params=_params("parallel", "parallel"),
        name="attention_bias",
    )(rel_table.reshape(h, 1, N_REL))


def _attention_body(heads, nkb, n_casts, q_ref, *refs):
    k_refs, v_refs = refs[:nkb], refs[nkb:2 * nkb]
    bias_ref = refs[2 * nkb]
    cast_in = refs[2 * nkb + 1:2 * nkb + 1 + n_casts]
    o_ref = refs[2 * nkb + 1 + n_casts]
    cast_out = refs[2 * nkb + 2 + n_casts:]
    for src, dst in zip(cast_in, cast_out):
        dst[...] = src[...].astype(dst.dtype)
    qb = q_ref.shape[0]
    exp2_scale = (HEAD_DIM ** -0.5) * LOG2_E

    def lanes(hh):
        return slice(hh * HEAD_DIM, (hh + 1) * HEAD_DIM)

    def logits(hh, _):
        q = q_ref[:, lanes(hh)]
        return [lax.dot_general(q, k_refs[j][:, lanes(hh)], (((1,), (1,)), ((), ())),
                                preferred_element_type=F32)
                + bias_ref[hh, :, j * qb:(j + 1) * qb] for j in range(nkb)]

    def weights(hh, raw):
        mx = jnp.max(functools.reduce(jnp.maximum, raw), axis=-1, keepdims=True)
        es = [jnp.exp2((s - mx) * exp2_scale) for s in raw]
        den = jnp.sum(functools.reduce(jnp.add, es), axis=-1, keepdims=True)
        return [e.astype(BF16) for e in es], den

    def values(hh, es_den):
        es, den = es_den
        acc = functools.reduce(jnp.add, [
            jnp.dot(es[j], v_refs[j][:, lanes(hh)], preferred_element_type=F32)
            for j in range(nkb)])
        o_ref[:, lanes(hh)] = (acc / den).astype(o_ref.dtype)

    _staggered(heads, (logits, weights, values))


def band_attention(proj, bias, batch, t, n_heads, col0, casts=()):
    m = batch * t
    nkb, _, qb, _ = bias.shape
    nq = t // qb
    hd = HEAD_DIM
    heads = _tile(n_heads, ATTENTION_HEADS)
    ng = n_heads // heads
    assert col0 % heads == 0
    cg0 = col0 // heads

    def kv_spec(which, j):
        back = nkb - 1 - j
        return pl.BlockSpec(
            (qb, heads * hd),
            lambda g, b, i: (b * nq + jnp.maximum(i - back, 0), cg0 + which * ng + g))

    in_specs = [pl.BlockSpec((qb, heads * hd), lambda g, b, i: (b * nq + i, cg0 + g))]
    in_specs += [kv_spec(1, j) for j in range(nkb)]
    in_specs += [kv_spec(2, j) for j in range(nkb)]
    in_specs += [pl.BlockSpec((None, heads, qb, LEFT + qb),
                              lambda g, b, i: (jnp.minimum(i, nkb - 1), g, 0, 0))]
    cast_in, cast_out, cast_shapes, cast_args = _cast_streams(
        casts, ng * batch * nq, lambda g, b, i: (g * batch + b) * nq + i)
    return pl.pallas_call(
        functools.partial(_attention_body, heads, nkb, len(casts)),
        grid=(ng, batch, nq),
        in_specs=in_specs + cast_in,
        out_specs=[pl.BlockSpec((qb, heads * hd), lambda g, b, i: (b * nq + i, g))] + cast_out,
        out_shape=[jax.ShapeDtypeStruct((m, n_heads * hd), BF16)] + cast_shapes,
        compiler_params=_params("arbitrary", "arbitrary", "arbitrary"),
        name="band_attention",
    )(*([proj] * (1 + 2 * nkb)), bias, *cast_args)


def _ffn_up_step(tiles_per_seq, nj, a_ref, ss_ref, wg_ref, wv_ref, cwg_ref, cwv_ref, cbg_ref,
                 cbv_ref, o_ref, cur_ref, prev_ref, halo_ref):
    tm = a_ref.shape[0]
    prev = jnp.maximum(pl.program_id(0) - 1, 0)
    jp = prev % nj
    seq_start = ((prev // nj) % tiles_per_seq) == 0

    above = [halo_ref[half, jp] for half in range(2)]
    for half in range(2):
        halo_ref[half, jp] = prev_ref[half, tm:, :]

    taps, biases = [], []
    for half, (w_ref, b_ref) in enumerate(((cwg_ref, cbg_ref), (cwv_ref, cbv_ref))):
        prev_ref[half, 0:SUBLANES, :] = jnp.where(seq_start, 0.0, above[half])
        w = w_ref[...]
        taps.append([w[k:k + 1] for k in range(CONV_WIDTH)])
        biases.append(b_ref[...])

    def conv(half, r0, rows):
        out = biases[half]
        for k in range(CONV_WIDTH):
            lo = SUBLANES + r0 - (CONV_WIDTH - 1 - k)
            out = out + taps[half][k] * prev_ref[half, lo:lo + rows, :]
        return out

    d = a_ref.shape[1]
    n_chunks = max(1, min(tm // FFN_EPILOGUE_ROWS, d // FFN_K_CHUNK))
    rows, kc = tm // n_chunks, d // n_chunks
    acc = [None, None]
    for c in range(n_chunks):
        a = a_ref[:, c * kc:(c + 1) * kc]
        for half, w_ref in enumerate((wg_ref, wv_ref)):
            w = w_ref[c * kc:(c + 1) * kc, :].astype(BF16)
            part = jnp.dot(a, w, preferred_element_type=F32)
            acc[half] = part if acc[half] is None else acc[half] + part
        r0 = c * rows
        g = conv(0, r0, rows)
        val = conv(1, r0, rows)
        o_ref[r0:r0 + rows, :] = (_silu(g) * val).astype(o_ref.dtype)
    scale = _row_scale(ss_ref, d)
    cur_ref[0, SUBLANES:, :] = acc[0] * scale
    cur_ref[1, SUBLANES:, :] = acc[1] * scale


def _ffn_up_body(tiles_per_seq, nj, *refs):
    io_refs, (raw_a, raw_b, halo_ref) = refs[:-3], refs[-3:]
    step = pl.program_id(0)
    run = functools.partial(_ffn_up_step, tiles_per_seq, nj, *io_refs)

    @pl.when(step == 0)
    def _():
        raw_b[...] = jnp.zeros_like(raw_b)
        halo_ref[...] = jnp.zeros_like(halo_ref)

    @pl.when(step % 2 == 0)
    def _():
        run(raw_a, raw_b, halo_ref)

    @pl.when(step % 2 == 1)
    def _():
        run(raw_b, raw_a, halo_ref)


def ffn_up(xg, ss, w, layer, conv_w, conv_b, t):
    m, d = xg.shape
    f = w.shape[2] // 2
    tm, tn = _tile(min(m, t), 1024), _tile(f, 2 * LANES)
    ni, nj = m // tm, f // tn
    n_tiles = ni * nj

    def cur(s):
        c = jnp.minimum(s, n_tiles - 1)
        return c // nj, c % nj

    def prev(s):
        p = jnp.maximum(s - 1, 0)
        return p // nj, p % nj

    def wspec(half):
        return pl.BlockSpec((None, d, tn), lambda s: (layer, 0, half * nj + cur(s)[1]))

    def cspec(rows, half):
        return pl.BlockSpec((rows, tn), lambda s: (0, half * nj + prev(s)[1]))

    raw = pltpu.VMEM((2, SUBLANES + tm, tn), F32)
    return pl.pallas_call(
        functools.partial(_ffn_up_body, t // tm, nj),
        grid=(n_tiles + 1,),
        in_specs=[pl.BlockSpec((tm, d), lambda s: (cur(s)[0], 0)),
                  _ss_in_spec(ss, tm, lambda s: cur(s)[0]), wspec(0), wspec(1),
                  cspec(CONV_WIDTH, 0), cspec(CONV_WIDTH, 1), cspec(1, 0), cspec(1, 1)],
        out_specs=pl.BlockSpec((tm, tn), lambda s: prev(s)),
        out_shape=jax.ShapeDtypeStruct((m, f), BF16),
        scratch_shapes=[raw, raw, pltpu.VMEM((2, nj, SUBLANES, tn), F32)],
        compiler_params=_params("arbitrary"),
        name="ffn_up",
    )(xg, ss, w, w, conv_w, conv_w, conv_b.reshape(1, 2 * f), conv_b.reshape(1, 2 * f))


def kernel(x, ln_mix, w_in, rel_bias, w_out, ln_ffn, w_up, conv_w, conv_b, w_down, ln_final):
    batch, t, d = x.shape
    depth = w_in.shape[0]
    m = batch * t
    n_ret = (d // 2) // HEAD_DIM
    n_att = rel_bias.shape[1]

    cos_t, sin_t = rope_tables(t)
    log_g = jnp.log(1.0 - 2.0 ** (-5.0 - jnp.arange(n_ret, dtype=F32)))
    ret_block = _tile(t, 256)
    att_block = _tile(LEFT, 256)

    h = x.reshape(m, d)
    xg, ss = prescale(h, ln_mix[0])
    w_in_b = cast_weight(w_in, 0)
    for layer in range(depth):
        proj = in_proj(xg, ss, w_in_b)
        ro, w_dn = retention(proj, cos_t, sin_t, log_g, batch, t, n_ret, ret_block,
                             casts=[(w_down, layer)])
        bias = attention_bias(rel_bias[layer], att_block)
        casts = [(w_out, layer)] + ([(w_in, layer + 1)] if layer + 1 < depth else [])
        ao, w_o, *w_next = band_attention(proj, bias, batch, t, n_att, 4 * n_ret, casts=casts)
        w_in_b = w_next[0] if w_next else None
        h, xg, ss = out_proj_residual(ro, ao, w_o, h, ln_ffn[layer])

        act = ffn_up(xg, ss, w_up, layer, conv_w[layer], conv_b[layer], t)
        if layer + 1 < depth:
            h, xg, ss = down_proj_residual(act, w_dn, h, ln_mix[layer + 1])
        else:
            h = down_proj_residual(act, w_dn, h)
    return rmsnorm(h, ln_final, F32).reshape(batch, t, d)
```

```python
import functools

import jax
import jax.numpy as jnp
from jax import lax
from jax.experimental import pallas as pl
from jax.experimental.pallas import tpu as pltpu

CHUNK = 64
HEAD_DIM = 128
LEFT_CHUNKS = 8
LEFT = LEFT_CHUNKS * CHUNK
REL_CLIP = 128
N_REL = REL_CLIP + CHUNK
CONV_WIDTH = 3
ROPE_BASE = 10000.0
EPS = 1e-6

LANES = 128
SUBLANES = 8
VMEM_LIMIT = 56 * 1024 * 1024
RETENTION_HEADS = 8
ATTENTION_HEADS = 8
FFN_EPILOGUE_ROWS = 64
FFN_K_CHUNK = 256
MASKED = -1e30
LOG2_E = 1.4426950408889634

F32 = jnp.float32
BF16 = jnp.bfloat16


def _params(*sem):
    return pltpu.CompilerParams(dimension_semantics=sem, vmem_limit_bytes=VMEM_LIMIT)


def _tile(n, want):
    t = min(n, want)
    while n % t:
        t //= 2
    return t


def _rmsnorm_body(x_ref, g_ref, o_ref):
    x = x_ref[...]
    ms = jnp.mean(x * x, axis=-1, keepdims=True)
    o_ref[...] = (x * lax.rsqrt(ms + EPS) * g_ref[...]).astype(o_ref.dtype)


def rmsnorm(x, gain, out_dtype):
    m, d = x.shape
    tm = _tile(m, 256)
    return pl.pallas_call(
        _rmsnorm_body,
        grid=(m // tm,),
        in_specs=[pl.BlockSpec((tm, d), lambda i: (i, 0)),
                  pl.BlockSpec((1, d), lambda i: (0, 0))],
        out_specs=pl.BlockSpec((tm, d), lambda i: (i, 0)),
        out_shape=jax.ShapeDtypeStruct((m, d), out_dtype),
        compiler_params=_params("parallel"),
        name="rmsnorm",
    )(x, gain.reshape(1, d))


def _sum_squares(x):
    return jnp.sum(x * x, axis=-1, keepdims=True)


def _row_scale(ss_ref, d):
    return lax.rsqrt(jnp.sum(ss_ref[...], axis=0) / d + EPS)


def _ss_in_spec(ss, tm, row_of):
    return pl.BlockSpec((ss.shape[0], tm, 1), lambda *idx: (0, row_of(*idx), 0))


def _prescale_body(x_ref, g_ref, o_ref, ss_ref):
    x = x_ref[...]
    o_ref[...] = (x * g_ref[...]).astype(o_ref.dtype)
    ss_ref[...] = _sum_squares(x)


def prescale(x, gain):
    m, d = x.shape
    tm = _tile(m, 256)
    return pl.pallas_call(
        _prescale_body,
        grid=(m // tm,),
        in_specs=[pl.BlockSpec((tm, d), lambda i: (i, 0)),
                  pl.BlockSpec((1, d), lambda i: (0, 0))],
        out_specs=[pl.BlockSpec((tm, d), lambda i: (i, 0)),
                   pl.BlockSpec((None, tm, 1), lambda i: (0, i, 0))],
        out_shape=[jax.ShapeDtypeStruct((m, d), BF16),
                   jax.ShapeDtypeStruct((1, m, 1), F32)],
        compiler_params=_params("parallel"),
        name="prescale",
    )(x, gain.reshape(1, d))


def _in_proj_body(a_ref, ss_ref, b_ref, o_ref):
    acc = jnp.dot(a_ref[...], b_ref[...], preferred_element_type=F32)
    o_ref[...] = (acc * _row_scale(ss_ref, a_ref.shape[1])).astype(o_ref.dtype)


def in_proj(a, ss, b):
    m, k = a.shape
    n = b.shape[1]
    tm, tn = _tile(m, 1024), _tile(n, 1024)
    return pl.pallas_call(
        _in_proj_body,
        grid=(m // tm, n // tn),
        in_specs=[pl.BlockSpec((tm, k), lambda i, j: (i, 0)),
                  _ss_in_spec(ss, tm, lambda i, j: i),
                  pl.BlockSpec((k, tn), lambda i, j: (0, j))],
        out_specs=pl.BlockSpec((tm, tn), lambda i, j: (i, j)),
        out_shape=jax.ShapeDtypeStruct((m, n), BF16),
        compiler_params=_params("parallel", "arbitrary"),
        name="in_proj",
    )(a, ss, b)


def _emit_residual(h, col_tile, g_ref, o_ref, hg_ref, ss_ref):
    o_ref[...] = h
    hg_ref[...] = (h * g_ref[...]).astype(hg_ref.dtype)
    part = _sum_squares(h)

    @pl.when(col_tile == 0)
    def _():
        ss_ref[...] = part

    @pl.when(col_tile > 0)
    def _():
        ss_ref[...] += part


def _residual_out(m, n, tm, tn, idx):
    specs = [pl.BlockSpec((tm, tn), lambda *g: idx(*g)),
             pl.BlockSpec((tm, tn), lambda *g: idx(*g)),
             pl.BlockSpec((None, tm, 1), lambda *g: (0, idx(*g)[0], 0))]
    shapes = [jax.ShapeDtypeStruct((m, n), F32), jax.ShapeDtypeStruct((m, n), BF16),
              jax.ShapeDtypeStruct((1, m, 1), F32)]
    return specs, shapes


def _out_proj_body(a1_ref, a2_ref, b1_ref, b2_ref, r_ref, g_ref, o_ref, hg_ref, ss_ref):
    acc = jnp.dot(a1_ref[...], b1_ref[...], preferred_element_type=F32)
    acc += jnp.dot(a2_ref[...], b2_ref[...], preferred_element_type=F32)
    _emit_residual(r_ref[...] + acc, pl.program_id(1), g_ref, o_ref, hg_ref, ss_ref)


def out_proj_residual(a1, a2, w, res, next_gain):
    m, k1 = a1.shape
    k2 = a2.shape[1]
    n = w.shape[1]
    assert k1 == k2 and w.shape[0] == k1 + k2
    tm, tn = _tile(m, 1024), _tile(n, 512)
    out_specs, out_shape = _residual_out(m, n, tm, tn, lambda i, j: (i, j))
    return pl.pallas_call(
        _out_proj_body,
        grid=(m // tm, n // tn),
        in_specs=[pl.BlockSpec((tm, k1), lambda i, j: (i, 0)),
                  pl.BlockSpec((tm, k2), lambda i, j: (i, 0)),
                  pl.BlockSpec((k1, tn), lambda i, j: (0, j)),
                  pl.BlockSpec((k2, tn), lambda i, j: (1, j)),
                  pl.BlockSpec((tm, tn), lambda i, j: (i, j)),
                  pl.BlockSpec((1, tn), lambda i, j: (0, j))],
        out_specs=out_specs,
        out_shape=out_shape,
        compiler_params=_params("parallel", "arbitrary"),
        name="out_proj",
    )(a1, a2, w, w, res, next_gain.reshape(1, n))


def _down_proj_body(nk, k_last, emit_next, a_ref, b_ref, r_ref, *refs):
    col_tile = pl.program_id(1)
    k = pl.program_id(2)
    acc_ref = refs[-1]

    @pl.when(k == 0)
    def _():
        acc_ref[...] = jnp.dot(a_ref[...], b_ref[...], preferred_element_type=F32)

    @pl.when((k > 0) & (k < nk - 1))
    def _():
        acc_ref[...] += jnp.dot(a_ref[...], b_ref[...], preferred_element_type=F32)

    @pl.when(k == nk - 1)
    def _():
        tail = jnp.dot(a_ref[:, :k_last], b_ref[:k_last, :], preferred_element_type=F32)
        h = r_ref[...] + (acc_ref[...] + tail)
        if emit_next:
            _emit_residual(h, col_tile, *refs[:4])
        else:
            refs[0][...] = h


def down_proj_residual(a, w, res, next_gain=None):
    m, k = a.shape
    n = w.shape[1]
    tm, tn = _tile(m, 1024), _tile(n, 1024)
    tk = -(-k // (4 * 2 * LANES)) * 2 * LANES
    nk = -(-k // tk)
    assert nk >= 2
    k_last = k - (nk - 1) * tk
    emit_next = next_gain is not None
    in_specs = [pl.BlockSpec((tm, tk), lambda i, j, kk: (i, kk)),
                pl.BlockSpec((tk, tn), lambda i, j, kk: (kk, j)),
                pl.BlockSpec((tm, tn), lambda i, j, kk: (i, j))]
    args = [a, w, res]
    out_specs, out_shape = _residual_out(m, n, tm, tn, lambda i, j, kk: (i, j))
    if emit_next:
        in_specs.append(pl.BlockSpec((1, tn), lambda i, j, kk: (0, j)))
        args.append(next_gain.reshape(1, n))
    else:
        out_specs, out_shape = out_specs[0], out_shape[0]
    return pl.pallas_call(
        functools.partial(_down_proj_body, nk, k_last, emit_next),
        grid=(m // tm, n // tn, nk),
        in_specs=in_specs,
        out_specs=out_specs,
        out_shape=out_shape,
        scratch_shapes=[pltpu.VMEM((tm, tn), F32)],
        compiler_params=_params("parallel", "arbitrary", "arbitrary"),
        name="down_proj",
    )(*args)


def _cast_body(x_ref, o_ref):
    o_ref[...] = x_ref[...].astype(o_ref.dtype)


def cast_weight(w, layer):
    _, k, n = w.shape
    tk, tn = _tile(k, 1024), _tile(n, 2048)
    return pl.pallas_call(
        _cast_body,
        grid=(k // tk, n // tn),
        in_specs=[pl.BlockSpec((None, tk, tn), lambda i, j: (layer, i, j))],
        out_specs=pl.BlockSpec((tk, tn), lambda i, j: (i, j)),
        out_shape=jax.ShapeDtypeStruct((k, n), BF16),
        compiler_params=_params("parallel", "parallel"),
        name="cast_weight",
    )(w)


def _cast_stream(w, layer, n_steps, step_of):
    _, k, n = w.shape
    rows = next(r for r in (2 * SUBLANES << p for p in range(16))
                if k % r == 0 and k // r <= n_steps)
    last = k // rows - 1

    def block(*g):
        return jnp.minimum(step_of(*g), last)

    return (pl.BlockSpec((None, rows, n), lambda *g: (layer, block(*g), 0)),
            pl.BlockSpec((rows, n), lambda *g: (block(*g), 0)),
            jax.ShapeDtypeStruct((k, n), BF16))


def _cast_streams(casts, n_steps, step_of):
    specs = [_cast_stream(w, layer, n_steps, step_of) for w, layer in casts]
    return ([s[0] for s in specs], [s[1] for s in specs], [s[2] for s in specs],
            [w for w, _ in casts])


def _rope_body(inv_ref, cos_ref, sin_ref):
    tt = cos_ref.shape[0]
    row = lax.broadcasted_iota(jnp.int32, (tt, LANES), 0) + pl.program_id(0) * tt
    lane = lax.broadcasted_iota(jnp.int32, (tt, LANES), 1)
    ang = row.astype(F32) * inv_ref[...]
    cos_ref[...] = jnp.cos(ang)
    sin_ref[...] = jnp.where(lane < HEAD_DIM // 2, -1.0, 1.0) * jnp.sin(ang)


def rope_tables(t):
    half = HEAD_DIM // 2
    inv = 1.0 / (ROPE_BASE ** jnp.linspace(0.0, 1.0, half, dtype=F32))
    inv2 = jnp.concatenate([inv, inv]).reshape(1, HEAD_DIM)
    tt = _tile(t, 512)
    return pl.pallas_call(
        _rope_body,
        grid=(t // tt,),
        in_specs=[pl.BlockSpec((1, HEAD_DIM), lambda i: (0, 0))],
        out_specs=[pl.BlockSpec((tt, HEAD_DIM), lambda i: (i, 0)),
                   pl.BlockSpec((tt, HEAD_DIM), lambda i: (i, 0))],
        out_shape=[jax.ShapeDtypeStruct((t, HEAD_DIM), F32)] * 2,
        compiler_params=_params("parallel"),
        name="rope_tables",
    )(inv2)


def _silu(x):
    return x * jax.nn.sigmoid(x)


def _staggered(n_items, stages):
    carried = [None] * n_items
    for t in range(n_items + len(stages) - 1):
        for s, stage in enumerate(stages):
            item = t - s
            if 0 <= item < n_items:
                carried[item] = stage(item, carried[item])


def _retention_body(heads, n_casts, logg_ref, q_ref, k_ref, v_ref, g_ref, cos_ref, sin_ref,
                    *refs):
    cast_in, o_ref, cast_out = refs[:n_casts], refs[n_casts], refs[n_casts + 1:2 * n_casts + 1]
    state_ref, intra_ref, qdec_ref, kdec_ref = refs[2 * n_casts + 1:]
    group = pl.program_id(1)
    step = pl.program_id(2)
    c = q_ref.shape[0]

    for src, dst in zip(cast_in, cast_out):
        dst[...] = src[...].astype(dst.dtype)

    @pl.when(step == 0)
    def _():
        state_ref[...] = jnp.zeros_like(state_ref)
        row = lax.broadcasted_iota(jnp.int32, (c, c), 0)
        col = lax.broadcasted_iota(jnp.int32, (c, c), 1)
        diff = (row - col).astype(F32)
        pos = lax.broadcasted_iota(jnp.int32, (c, HEAD_DIM), 0).astype(F32)
        for hh in range(heads):
            lg = logg_ref[group * heads + hh]
            intra_ref[hh] = jnp.where(diff >= 0, jnp.exp(jnp.maximum(diff, 0.0) * lg), 0.0)
            qdec_ref[hh] = jnp.exp((pos + 1.0) * lg)
            kdec_ref[hh] = jnp.exp((c - 1.0 - pos) * lg)

    cos = cos_ref[...]
    sin = sin_ref[...]

    def rot(x):
        return x * cos + pltpu.roll(x, HEAD_DIM // 2, axis=1) * sin

    def lanes(hh):
        return slice(hh * HEAD_DIM, (hh + 1) * HEAD_DIM)

    def rotate(hh, _):
        q = rot(q_ref[:, lanes(hh)].astype(F32))
        k = rot(k_ref[:, lanes(hh)].astype(F32)) * (HEAD_DIM ** -0.5)
        return q.astype(BF16), k.astype(BF16), (k * kdec_ref[hh]).astype(BF16)

    def scores(hh, qk):
        qb, kb, kd = qk
        s = lax.dot_general(qb, kb, (((1,), (1,)), ((), ())), preferred_element_type=F32)
        return qb, kd, (s * intra_ref[hh]).astype(BF16)

    def outputs(hh, qks):
        qb, kd, s = qks
        v = v_ref[:, lanes(hh)]
        state = state_ref[hh]
        o = jnp.dot(s, v, preferred_element_type=F32)
        o += jnp.dot(qb, state.astype(BF16), preferred_element_type=F32) * qdec_ref[hh]
        kv = lax.dot_general(kd, v, (((0,), (0,)), ((), ())), preferred_element_type=F32)
        block_decay = jnp.exp(jnp.full((1, HEAD_DIM), float(c), F32)
                              * logg_ref[group * heads + hh])
        state_ref[hh] = state * block_decay + kv
        return o

    def norm_gate(hh, o):
        o = o * lax.rsqrt(jnp.mean(o * o, axis=-1, keepdims=True) + EPS)
        o_ref[:, lanes(hh)] = (_silu(g_ref[:, lanes(hh)].astype(F32)) * o).astype(o_ref.dtype)

    _staggered(heads, (rotate, scores, outputs, norm_gate))


def retention(proj, cos_t, sin_t, log_g, batch, t, n_heads, block, casts=()):
    m = batch * t
    nt = t // block
    hd = HEAD_DIM
    heads = _tile(n_heads, RETENTION_HEADS)
    ng = n_heads // heads
    cast_in, cast_out, cast_shapes, cast_args = _cast_streams(
        casts, batch * ng * nt, lambda b, g, i, lg: (b * ng + g) * nt + i)

    def col(which):
        return pl.BlockSpec((block, heads * hd),
                            lambda b, g, i, lg: (b * nt + i, which * ng + g))

    tab = pl.BlockSpec((block, hd), lambda b, g, i, lg: (i, 0))
    return pl.pallas_call(
        functools.partial(_retention_body, heads, len(casts)),
        grid_spec=pltpu.PrefetchScalarGridSpec(
            num_scalar_prefetch=1,
            grid=(batch, ng, nt),
            in_specs=[col(0), col(1), col(2), col(3), tab, tab] + cast_in,
            out_specs=[pl.BlockSpec((block, heads * hd),
                                    lambda b, g, i, lg: (b * nt + i, g))] + cast_out,
            scratch_shapes=[pltpu.VMEM((heads, hd, hd), F32),
                            pltpu.VMEM((heads, block, block), F32),
                            pltpu.VMEM((heads, block, hd), F32),
                            pltpu.VMEM((heads, block, hd), F32)],
        ),
        out_shape=[jax.ShapeDtypeStruct((m, n_heads * hd), BF16)] + cast_shapes,
        compiler_params=_params("arbitrary", "arbitrary", "arbitrary"),
        name="retention",
    )(log_g, proj, proj, proj, proj, cos_t, sin_t, *cast_args)


def _att_bias_body(tab_ref, o_ref):
    qb, width = o_ref.shape
    n = pl.cdiv(width + qb - 1, LANES) * LANES
    tab = jnp.broadcast_to(tab_ref[0], (SUBLANES, N_REL))
    u = lax.broadcasted_iota(jnp.int32, (N_REL, n), 1)
    kk = lax.broadcasted_iota(jnp.int32, (N_REL, n), 0)
    rel = jnp.clip(LEFT + qb - 1 - u, -(CHUNK - 1), REL_CLIP) + (CHUNK - 1)
    onehot = jnp.where(rel == kk, 1.0, 0.0).astype(BF16)
    e = jnp.zeros((SUBLANES, n), F32)
    rest = tab
    for _ in range(3):
        piece = rest.astype(BF16)
        e += jnp.dot(piece, onehot, preferred_element_type=F32)
        rest = rest - piece.astype(F32)
    eb = jnp.broadcast_to(e[0:1], (qb, n))
    bias = pltpu.roll(eb, n - (qb - 1), axis=1, stride=1, stride_axis=0)[:, :width]
    r = lax.broadcasted_iota(jnp.int32, (qb, width), 0) // CHUNK
    mc = lax.broadcasted_iota(jnp.int32, (qb, width), 1) // CHUNK
    blocks_before_start = (width // qb - 1) - pl.program_id(0)
    visible = (mc >= r) & (mc <= r + LEFT_CHUNKS) & (mc * CHUNK >= blocks_before_start * qb)
    o_ref[...] = jnp.where(visible, bias * (HEAD_DIM ** 0.5), MASKED)


def attention_bias(rel_table, qb):
    h = rel_table.shape[0]
    width = LEFT + qb
    nkb = width // qb
    return pl.pallas_call(
        _att_bias_body,
        grid=(nkb, h),
        in_specs=[pl.BlockSpec((1, 1, N_REL), lambda v, i: (i, 0, 0))],
        out_specs=pl.BlockSpec((None, None, qb, width), lambda v, i: (v, i, 0, 0)),
        out_shape=jax.ShapeDtypeStruct((nkb, h, qb, width), F32),
        compiler_params=_params("parallel", "parallel"),
        name="attention_bias",
    )(rel_table.reshape(h, 1, N_REL))


def _attention_body(heads, nkb, n_casts, q_ref, *refs):
    k_refs, v_refs = refs[:nkb], refs[nkb:2 * nkb]
    bias_ref = refs[2 * nkb]
    cast_in = refs[2 * nkb + 1:2 * nkb + 1 + n_casts]
    o_ref = refs[2 * nkb + 1 + n_casts]
    cast_out = refs[2 * nkb + 2 + n_casts:]
    for src, dst in zip(cast_in, cast_out):
        dst[...] = src[...].astype(dst.dtype)
    qb = q_ref.shape[0]
    exp2_scale = (HEAD_DIM ** -0.5) * LOG2_E

    def lanes(hh):
        return slice(hh * HEAD_DIM, (hh + 1) * HEAD_DIM)

    def logits(hh, _):
        q = q_ref[:, lanes(hh)]
        return [lax.dot_general(q, k_refs[j][:, lanes(hh)], (((1,), (1,)), ((), ())),
                                preferred_element_type=F32)
                + bias_ref[hh, :, j * qb:(j + 1) * qb] for j in range(nkb)]

    def weights(hh, raw):
        mx = jnp.max(functools.reduce(jnp.maximum, raw), axis=-1, keepdims=True)
        es = [jnp.exp2((s - mx) * exp2_scale) for s in raw]
        den = jnp.sum(functools.reduce(jnp.add, es), axis=-1, keepdims=True)
        return [e.astype(BF16) for e in es], den

    def values(hh, es_den):
        es, den = es_den
        acc = functools.reduce(jnp.add, [
            jnp.dot(es[j], v_refs[j][:, lanes(hh)], preferred_element_type=F32)
            for j in range(nkb)])
        o_ref[:, lanes(hh)] = (acc / den).astype(o_ref.dtype)

    _staggered(heads, (logits, weights, values))


def band_attention(proj, bias, batch, t, n_heads, col0, casts=()):
    m = batch * t
    nkb, _, qb, _ = bias.shape
    nq = t // qb
    hd = HEAD_DIM
    heads = _tile(n_heads, ATTENTION_HEADS)
    ng = n_heads // heads
    assert col0 % heads == 0
    cg0 = col0 // heads

    def kv_spec(which, j):
        back = nkb - 1 - j
        return pl.BlockSpec(
            (qb, heads * hd),
            lambda g, b, i: (b * nq + jnp.maximum(i - back, 0), cg0 + which * ng + g))

    in_specs = [pl.BlockSpec((qb, heads * hd), lambda g, b, i: (b * nq + i, cg0 + g))]
    in_specs += [kv_spec(1, j) for j in range(nkb)]
    in_specs += [kv_spec(2, j) for j in range(nkb)]
    in_specs += [pl.BlockSpec((None, heads, qb, LEFT + qb),
                              lambda g, b, i: (jnp.minimum(i, nkb - 1), g, 0, 0))]
    cast_in, cast_out, cast_shapes, cast_args = _cast_streams(
        casts, ng * batch * nq, lambda g, b, i: (g * batch + b) * nq + i)
    return pl.pallas_call(
        functools.partial(_attention_body, heads, nkb, len(casts)),
        grid=(ng, batch, nq),
        in_specs=in_specs + cast_in,
        out_specs=[pl.BlockSpec((qb, heads * hd), lambda g, b, i: (b * nq + i, g))] + cast_out,
        out_shape=[jax.ShapeDtypeStruct((m, n_heads * hd), BF16)] + cast_shapes,
        compiler_params=_params("arbitrary", "arbitrary", "arbitrary"),
        name="band_attention",
    )(*([proj] * (1 + 2 * nkb)), bias, *cast_args)


def _ffn_up_step(tiles_per_seq, ni, a_ref, ss_ref, cwg_ref, cwv_ref, cbg_ref, cbv_ref, o_ref,
                 wb_ref, cur_ref, prev_ref, halo_ref):
    tm = a_ref.shape[0]
    prev = jnp.maximum(pl.program_id(0) - 1, 0)
    seq_start = ((prev % ni) % tiles_per_seq) == 0

    above = [halo_ref[half] for half in range(2)]
    for half in range(2):
        halo_ref[half] = prev_ref[half, tm:, :]

    taps, biases = [], []
    for half, (w_ref, b_ref) in enumerate(((cwg_ref, cbg_ref), (cwv_ref, cbv_ref))):
        prev_ref[half, 0:SUBLANES, :] = jnp.where(seq_start, 0.0, above[half])
        w = w_ref[...]
        taps.append([w[k:k + 1] for k in range(CONV_WIDTH)])
        biases.append(b_ref[...])

    def conv(half, r0, rows):
        out = biases[half]
        for k in range(CONV_WIDTH):
            lo = SUBLANES + r0 - (CONV_WIDTH - 1 - k)
            out = out + taps[half][k] * prev_ref[half, lo:lo + rows, :]
        return out

    d = a_ref.shape[1]
    n_chunks = max(1, min(tm // FFN_EPILOGUE_ROWS, d // FFN_K_CHUNK))
    rows, kc = tm // n_chunks, d // n_chunks
    acc = [None, None]
    for c in range(n_chunks):
        a = a_ref[:, c * kc:(c + 1) * kc]
        for half in range(2):
            part = jnp.dot(a, wb_ref[half, c * kc:(c + 1) * kc, :],
                           preferred_element_type=F32)
            acc[half] = part if acc[half] is None else acc[half] + part
        r0 = c * rows
        g = conv(0, r0, rows)
        val = conv(1, r0, rows)
        o_ref[r0:r0 + rows, :] = (_silu(g) * val).astype(o_ref.dtype)
    scale = _row_scale(ss_ref, d)
    cur_ref[0, SUBLANES:, :] = acc[0] * scale
    cur_ref[1, SUBLANES:, :] = acc[1] * scale


def _ffn_up_body(tiles_per_seq, ni, n_tiles, a_ref, ss_ref, wg_ref, wv_ref, *refs):
    conv_and_out, (wb_ref, raw_a, raw_b, halo_ref) = refs[:-4], refs[-4:]
    step = pl.program_id(0)
    run = functools.partial(_ffn_up_step, tiles_per_seq, ni, a_ref, ss_ref, *conv_and_out,
                            wb_ref)

    @pl.when(step == 0)
    def _():
        raw_b[...] = jnp.zeros_like(raw_b)
        halo_ref[...] = jnp.zeros_like(halo_ref)

    @pl.when((step % ni == 0) & (step < n_tiles))
    def _():
        wb_ref[0] = wg_ref[...].astype(BF16)
        wb_ref[1] = wv_ref[...].astype(BF16)

    @pl.when(step % 2 == 0)
    def _():
        run(raw_a, raw_b, halo_ref)

    @pl.when(step % 2 == 1)
    def _():
        run(raw_b, raw_a, halo_ref)


def ffn_up(xg, ss, w, layer, conv_w, conv_b, t):
    m, d = xg.shape
    f = w.shape[2] // 2
    tm, tn = _tile(min(m, t), 1024), _tile(f, 2 * LANES)
    ni, nj = m // tm, f // tn
    n_tiles = ni * nj

    def cur(s):
        c = jnp.minimum(s, n_tiles - 1)
        return c % ni, c // ni

    def prev(s):
        p = jnp.maximum(s - 1, 0)
        return p % ni, p // ni

    def wspec(half):
        return pl.BlockSpec((None, d, tn), lambda s: (layer, 0, half * nj + cur(s)[1]))

    def cspec(rows, half):
        return pl.BlockSpec((rows, tn), lambda s: (0, half * nj + prev(s)[1]))

    raw = pltpu.VMEM((2, SUBLANES + tm, tn), F32)
    return pl.pallas_call(
        functools.partial(_ffn_up_body, t // tm, ni, n_tiles),
        grid=(n_tiles + 1,),
        in_specs=[pl.BlockSpec((tm, d), lambda s: (cur(s)[0], 0)),
                  _ss_in_spec(ss, tm, lambda s: cur(s)[0]), wspec(0), wspec(1),
                  cspec(CONV_WIDTH, 0), cspec(CONV_WIDTH, 1), cspec(1, 0), cspec(1, 1)],
        out_specs=pl.BlockSpec((tm, tn), lambda s: prev(s)),
        out_shape=jax.ShapeDtypeStruct((m, f), BF16),
        scratch_shapes=[pltpu.VMEM((2, d, tn), BF16), raw, raw,
                        pltpu.VMEM((2, SUBLANES, tn), F32)],
        compiler_params=_params("arbitrary"),
        name="ffn_up",
    )(xg, ss, w, w, conv_w, conv_w, conv_b.reshape(1, 2 * f), conv_b.reshape(1, 2 * f))


def kernel(x, ln_mix, w_in, rel_bias, w_out, ln_ffn, w_up, conv_w, conv_b, w_down, ln_final):
    batch, t, d = x.shape
    depth = w_in.shape[0]
    m = batch * t
    n_ret = (d // 2) // HEAD_DIM
    n_att = rel_bias.shape[1]

    cos_t, sin_t = rope_tables(t)
    log_g = jnp.log(1.0 - 2.0 ** (-5.0 - jnp.arange(n_ret, dtype=F32)))
    ret_block = _tile(t, 256)
    att_block = _tile(LEFT, 256)

    h = x.reshape(m, d)
    xg, ss = prescale(h, ln_mix[0])
    w_in_b = cast_weight(w_in, 0)
    for layer in range(depth):
        proj = in_proj(xg, ss, w_in_b)
        ro, w_dn = retention(proj, cos_t, sin_t, log_g, batch, t, n_ret, ret_block,
                             casts=[(w_down, layer)])
        bias = attention_bias(rel_bias[layer], att_block)
        casts = [(w_out, layer)] + ([(w_in, layer + 1)] if layer + 1 < depth else [])
        ao, w_o, *w_next = band_attention(proj, bias, batch, t, n_att, 4 * n_ret, casts=casts)
        w_in_b = w_next[0] if w_next else None
        h, xg, ss = out_proj_residual(ro, ao, w_o, h, ln_ffn[layer])

        act = ffn_up(xg, ss, w_up, layer, conv_w[layer], conv_b[layer], t)
        if layer + 1 < depth:
            h, xg, ss = down_proj_residual(act, w_dn, h, ln_mix[layer + 1])
        else:
            h = down_proj_residual(act, w_dn, h)
    return rmsnorm(h, ln_final, F32).reshape(batch, t, d)
```

```python
import functools

import jax
import jax.numpy as jnp
from jax import lax
from jax.experimental import pallas as pl
from jax.experimental.pallas import tpu as pltpu

CHUNK = 64
HEAD_DIM = 128
LEFT_CHUNKS = 8
LEFT = LEFT_CHUNKS * CHUNK
REL_CLIP = 128
N_REL = REL_CLIP + CHUNK
CONV_WIDTH = 3
ROPE_BASE = 10000.0
EPS = 1e-6

LANES = 128
SUBLANES = 8
VMEM_LIMIT = 56 * 1024 * 1024
RETENTION_HEADS = 8
ATTENTION_HEADS = 8
FFN_EPILOGUE_ROWS = 64
FFN_K_CHUNK = 256
MASKED = -1e30
LOG2_E = 1.4426950408889634

F32 = jnp.float32
BF16 = jnp.bfloat16


def _params(*sem):
    return pltpu.CompilerParams(dimension_semantics=sem, vmem_limit_bytes=VMEM_LIMIT)


def _tile(n, want):
    t = min(n, want)
    while n % t:
        t //= 2
    return t


def _rmsnorm_body(x_ref, g_ref, o_ref):
    x = x_ref[...]
    ms = jnp.mean(x * x, axis=-1, keepdims=True)
    o_ref[...] = (x * lax.rsqrt(ms + EPS) * g_ref[...]).astype(o_ref.dtype)


def rmsnorm(x, gain, out_dtype):
    m, d = x.shape
    tm = _tile(m, 256)
    return pl.pallas_call(
        _rmsnorm_body,
        grid=(m // tm,),
        in_specs=[pl.BlockSpec((tm, d), lambda i: (i, 0)),
                  pl.BlockSpec((1, d), lambda i: (0, 0))],
        out_specs=pl.BlockSpec((tm, d), lambda i: (i, 0)),
        out_shape=jax.ShapeDtypeStruct((m, d), out_dtype),
        compiler_params=_params("parallel"),
        name="rmsnorm",
    )(x, gain.reshape(1, d))


def _sum_squares(x):
    return jnp.sum(x * x, axis=-1, keepdims=True)


def _row_scale(ss_ref, d):
    return lax.rsqrt(jnp.sum(ss_ref[...], axis=0) / d + EPS)


def _ss_in_spec(ss, tm, row_of):
    return pl.BlockSpec((ss.shape[0], tm, 1), lambda *idx: (0, row_of(*idx), 0))


def _prescale_body(x_ref, g_ref, o_ref, ss_ref):
    x = x_ref[...]
    o_ref[...] = (x * g_ref[...]).astype(o_ref.dtype)
    ss_ref[...] = _sum_squares(x)


def prescale(x, gain):
    m, d = x.shape
    tm = _tile(m, 256)
    return pl.pallas_call(
        _prescale_body,
        grid=(m // tm,),
        in_specs=[pl.BlockSpec((tm, d), lambda i: (i, 0)),
                  pl.BlockSpec((1, d), lambda i: (0, 0))],
        out_specs=[pl.BlockSpec((tm, d), lambda i: (i, 0)),
                   pl.BlockSpec((None, tm, 1), lambda i: (0, i, 0))],
        out_shape=[jax.ShapeDtypeStruct((m, d), BF16),
                   jax.ShapeDtypeStruct((1, m, 1), F32)],
        compiler_params=_params("parallel"),
        name="prescale",
    )(x, gain.reshape(1, d))


def _in_proj_body(n_casts, a_ref, ss_ref, b_ref, *refs):
    cast_in, o_ref, cast_out = refs[:n_casts], refs[n_casts], refs[n_casts + 1:]
    for src, dst in zip(cast_in, cast_out):
        dst[...] = src[...].astype(dst.dtype)
    acc = jnp.dot(a_ref[...], b_ref[...], preferred_element_type=F32)
    o_ref[...] = (acc * _row_scale(ss_ref, a_ref.shape[1])).astype(o_ref.dtype)


def in_proj(a, ss, b, casts=()):
    m, k = a.shape
    n = b.shape[1]
    tm, tn = _tile(m, 1024), _tile(n, 1024)
    nj = n // tn
    cast_in, cast_out, cast_shapes, cast_args = _cast_streams(
        casts, (m // tm) * nj, lambda i, j: i * nj + j)
    return pl.pallas_call(
        functools.partial(_in_proj_body, len(casts)),
        grid=(m // tm, nj),
        in_specs=[pl.BlockSpec((tm, k), lambda i, j: (i, 0)),
                  _ss_in_spec(ss, tm, lambda i, j: i),
                  pl.BlockSpec((k, tn), lambda i, j: (0, j))] + cast_in,
        out_specs=[pl.BlockSpec((tm, tn), lambda i, j: (i, j))] + cast_out,
        out_shape=[jax.ShapeDtypeStruct((m, n), BF16)] + cast_shapes,
        compiler_params=_params("arbitrary", "arbitrary"),
        name="in_proj",
    )(a, ss, b, *cast_args)


def _emit_residual(h, col_tile, g_ref, o_ref, hg_ref, ss_ref):
    o_ref[...] = h
    hg_ref[...] = (h * g_ref[...]).astype(hg_ref.dtype)
    part = _sum_squares(h)

    @pl.when(col_tile == 0)
    def _():
        ss_ref[...] = part

    @pl.when(col_tile > 0)
    def _():
        ss_ref[...] += part


def _residual_out(m, n, tm, tn, idx):
    specs = [pl.BlockSpec((tm, tn), lambda *g: idx(*g)),
             pl.BlockSpec((tm, tn), lambda *g: idx(*g)),
             pl.BlockSpec((None, tm, 1), lambda *g: (0, idx(*g)[0], 0))]
    shapes = [jax.ShapeDtypeStruct((m, n), F32), jax.ShapeDtypeStruct((m, n), BF16),
              jax.ShapeDtypeStruct((1, m, 1), F32)]
    return specs, shapes


def _out_proj_body(a1_ref, a2_ref, b1_ref, b2_ref, r_ref, g_ref, o_ref, hg_ref, ss_ref):
    acc = jnp.dot(a1_ref[...], b1_ref[...], preferred_element_type=F32)
    acc += jnp.dot(a2_ref[...], b2_ref[...], preferred_element_type=F32)
    _emit_residual(r_ref[...] + acc, pl.program_id(1), g_ref, o_ref, hg_ref, ss_ref)


def out_proj_residual(a1, a2, w, res, next_gain):
    m, k1 = a1.shape
    k2 = a2.shape[1]
    n = w.shape[1]
    assert k1 == k2 and w.shape[0] == k1 + k2
    tm, tn = _tile(m, 1024), _tile(n, 512)
    out_specs, out_shape = _residual_out(m, n, tm, tn, lambda i, j: (i, j))
    return pl.pallas_call(
        _out_proj_body,
        grid=(m // tm, n // tn),
        in_specs=[pl.BlockSpec((tm, k1), lambda i, j: (i, 0)),
                  pl.BlockSpec((tm, k2), lambda i, j: (i, 0)),
                  pl.BlockSpec((k1, tn), lambda i, j: (0, j)),
                  pl.BlockSpec((k2, tn), lambda i, j: (1, j)),
                  pl.BlockSpec((tm, tn), lambda i, j: (i, j)),
                  pl.BlockSpec((1, tn), lambda i, j: (0, j))],
        out_specs=out_specs,
        out_shape=out_shape,
        compiler_params=_params("parallel", "arbitrary"),
        name="out_proj",
    )(a1, a2, w, w, res, next_gain.reshape(1, n))


def _down_proj_body(nk, k_last, emit_next, a_ref, b_ref, r_ref, *refs):
    col_tile = pl.program_id(1)
    k = pl.program_id(2)
    acc_ref = refs[-1]

    @pl.when(k == 0)
    def _():
        acc_ref[...] = jnp.dot(a_ref[...], b_ref[...], preferred_element_type=F32)

    @pl.when((k > 0) & (k < nk - 1))
    def _():
        acc_ref[...] += jnp.dot(a_ref[...], b_ref[...], preferred_element_type=F32)

    @pl.when(k == nk - 1)
    def _():
        tail = jnp.dot(a_ref[:, :k_last], b_ref[:k_last, :], preferred_element_type=F32)
        h = r_ref[...] + (acc_ref[...] + tail)
        if emit_next:
            _emit_residual(h, col_tile, *refs[:4])
        else:
            refs[0][...] = h


def down_proj_residual(a, w, res, next_gain=None):
    m, k = a.shape
    n = w.shape[1]
    tm, tn = _tile(m, 1024), _tile(n, 1024)
    tk = -(-k // (4 * 2 * LANES)) * 2 * LANES
    nk = -(-k // tk)
    assert nk >= 2
    k_last = k - (nk - 1) * tk
    emit_next = next_gain is not None
    in_specs = [pl.BlockSpec((tm, tk), lambda i, j, kk: (i, kk)),
                pl.BlockSpec((tk, tn), lambda i, j, kk: (kk, j)),
                pl.BlockSpec((tm, tn), lambda i, j, kk: (i, j))]
    args = [a, w, res]
    out_specs, out_shape = _residual_out(m, n, tm, tn, lambda i, j, kk: (i, j))
    if emit_next:
        in_specs.append(pl.BlockSpec((1, tn), lambda i, j, kk: (0, j)))
        args.append(next_gain.reshape(1, n))
    else:
        out_specs, out_shape = out_specs[0], out_shape[0]
    return pl.pallas_call(
        functools.partial(_down_proj_body, nk, k_last, emit_next),
        grid=(m // tm, n // tn, nk),
        in_specs=in_specs,
        out_specs=out_specs,
        out_shape=out_shape,
        scratch_shapes=[pltpu.VMEM((tm, tn), F32)],
        compiler_params=_params("parallel", "arbitrary", "arbitrary"),
        name="down_proj",
    )(*args)


def _cast_body(x_ref, o_ref):
    o_ref[...] = x_ref[...].astype(o_ref.dtype)


def cast_weight(w, layer):
    _, k, n = w.shape
    tk, tn = _tile(k, 1024), _tile(n, 2048)
    return pl.pallas_call(
        _cast_body,
        grid=(k // tk, n // tn),
        in_specs=[pl.BlockSpec((None, tk, tn), lambda i, j: (layer, i, j))],
        out_specs=pl.BlockSpec((tk, tn), lambda i, j: (i, j)),
        out_shape=jax.ShapeDtypeStruct((k, n), BF16),
        compiler_params=_params("parallel", "parallel"),
        name="cast_weight",
    )(w)


def _cast_stream(w, layer, n_steps, step_of):
    _, k, n = w.shape
    rows = next(r for r in (2 * SUBLANES << p for p in range(16))
                if k % r == 0 and k // r <= n_steps)
    last = k // rows - 1

    def block(*g):
        return jnp.minimum(step_of(*g), last)

    return (pl.BlockSpec((None, rows, n), lambda *g: (layer, block(*g), 0)),
            pl.BlockSpec((rows, n), lambda *g: (block(*g), 0)),
            jax.ShapeDtypeStruct((k, n), BF16))


def _cast_streams(casts, n_steps, step_of):
    specs = [_cast_stream(w, layer, n_steps, step_of) for w, layer in casts]
    return ([s[0] for s in specs], [s[1] for s in specs], [s[2] for s in specs],
            [w for w, _ in casts])


def _rope_body(inv_ref, cos_ref, sin_ref):
    tt = cos_ref.shape[0]
    row = lax.broadcasted_iota(jnp.int32, (tt, LANES), 0) + pl.program_id(0) * tt
    lane = lax.broadcasted_iota(jnp.int32, (tt, LANES), 1)
    ang = row.astype(F32) * inv_ref[...]
    cos_ref[...] = jnp.cos(ang)
    sin_ref[...] = jnp.where(lane < HEAD_DIM // 2, -1.0, 1.0) * jnp.sin(ang)


def rope_tables(t):
    half = HEAD_DIM // 2
    inv = 1.0 / (ROPE_BASE ** jnp.linspace(0.0, 1.0, half, dtype=F32))
    inv2 = jnp.concatenate([inv, inv]).reshape(1, HEAD_DIM)
    tt = _tile(t, 512)
    return pl.pallas_call(
        _rope_body,
        grid=(t // tt,),
        in_specs=[pl.BlockSpec((1, HEAD_DIM), lambda i: (0, 0))],
        out_specs=[pl.BlockSpec((tt, HEAD_DIM), lambda i: (i, 0)),
                   pl.BlockSpec((tt, HEAD_DIM), lambda i: (i, 0))],
        out_shape=[jax.ShapeDtypeStruct((t, HEAD_DIM), F32)] * 2,
        compiler_params=_params("parallel"),
        name="rope_tables",
    )(inv2)


def _silu(x):
    return x * jax.nn.sigmoid(x)


def _staggered(n_items, stages):
    carried = [None] * n_items
    for t in range(n_items + len(stages) - 1):
        for s, stage in enumerate(stages):
            item = t - s
            if 0 <= item < n_items:
                carried[item] = stage(item, carried[item])


def _retention_body(heads, n_casts, logg_ref, q_ref, k_ref, v_ref, g_ref, cos_ref, sin_ref,
                    *refs):
    cast_in, o_ref, cast_out = refs[:n_casts], refs[n_casts], refs[n_casts + 1:2 * n_casts + 1]
    state_ref, intra_ref, qdec_ref, kdec_ref = refs[2 * n_casts + 1:]
    group = pl.program_id(1)
    step = pl.program_id(2)
    c = q_ref.shape[0]

    for src, dst in zip(cast_in, cast_out):
        dst[...] = src[...].astype(dst.dtype)

    @pl.when(step == 0)
    def _():
        state_ref[...] = jnp.zeros_like(state_ref)
        row = lax.broadcasted_iota(jnp.int32, (c, c), 0)
        col = lax.broadcasted_iota(jnp.int32, (c, c), 1)
        diff = (row - col).astype(F32)
        pos = lax.broadcasted_iota(jnp.int32, (c, HEAD_DIM), 0).astype(F32)
        for hh in range(heads):
            lg = logg_ref[group * heads + hh]
            intra_ref[hh] = jnp.where(diff >= 0, jnp.exp(jnp.maximum(diff, 0.0) * lg), 0.0)
            qdec_ref[hh] = jnp.exp((pos + 1.0) * lg)
            kdec_ref[hh] = jnp.exp((c - 1.0 - pos) * lg)

    cos = cos_ref[...]
    sin = sin_ref[...]

    def rot(x):
        return x * cos + pltpu.roll(x, HEAD_DIM // 2, axis=1) * sin

    def lanes(hh):
        return slice(hh * HEAD_DIM, (hh + 1) * HEAD_DIM)

    def rotate(hh, _):
        q = rot(q_ref[:, lanes(hh)].astype(F32))
        k = rot(k_ref[:, lanes(hh)].astype(F32)) * (HEAD_DIM ** -0.5)
        return q.astype(BF16), k.astype(BF16), (k * kdec_ref[hh]).astype(BF16)

    def scores(hh, qk):
        qb, kb, kd = qk
        s = lax.dot_general(qb, kb, (((1,), (1,)), ((), ())), preferred_element_type=F32)
        return qb, kd, (s * intra_ref[hh]).astype(BF16)

    def outputs(hh, qks):
        qb, kd, s = qks
        v = v_ref[:, lanes(hh)]
        state = state_ref[hh]
        o = jnp.dot(s, v, preferred_element_type=F32)
        o += jnp.dot(qb, state.astype(BF16), preferred_element_type=F32) * qdec_ref[hh]
        kv = lax.dot_general(kd, v, (((0,), (0,)), ((), ())), preferred_element_type=F32)
        block_decay = jnp.exp(jnp.full((1, HEAD_DIM), float(c), F32)
                              * logg_ref[group * heads + hh])
        state_ref[hh] = state * block_decay + kv
        return o

    def norm_gate(hh, o):
        o = o * lax.rsqrt(jnp.mean(o * o, axis=-1, keepdims=True) + EPS)
        o_ref[:, lanes(hh)] = (_silu(g_ref[:, lanes(hh)].astype(F32)) * o).astype(o_ref.dtype)

    _staggered(heads, (rotate, scores, outputs, norm_gate))


def retention(proj, cos_t, sin_t, log_g, batch, t, n_heads, block, casts=()):
    m = batch * t
    nt = t // block
    hd = HEAD_DIM
    heads = _tile(n_heads, RETENTION_HEADS)
    ng = n_heads // heads
    cast_in, cast_out, cast_shapes, cast_args = _cast_streams(
        casts, batch * ng * nt, lambda b, g, i, lg: (b * ng + g) * nt + i)

    def col(which):
        return pl.BlockSpec((block, heads * hd),
                            lambda b, g, i, lg: (b * nt + i, which * ng + g))

    tab = pl.BlockSpec((block, hd), lambda b, g, i, lg: (i, 0))
    return pl.pallas_call(
        functools.partial(_retention_body, heads, len(casts)),
        grid_spec=pltpu.PrefetchScalarGridSpec(
            num_scalar_prefetch=1,
            grid=(batch, ng, nt),
            in_specs=[col(0), col(1), col(2), col(3), tab, tab] + cast_in,
            out_specs=[pl.BlockSpec((block, heads * hd),
                                    lambda b, g, i, lg: (b * nt + i, g))] + cast_out,
            scratch_shapes=[pltpu.VMEM((heads, hd, hd), F32),
                            pltpu.VMEM((heads, block, block), F32),
                            pltpu.VMEM((heads, block, hd), F32),
                            pltpu.VMEM((heads, block, hd), F32)],
        ),
        out_shape=[jax.ShapeDtypeStruct((m, n_heads * hd), BF16)] + cast_shapes,
        compiler_params=_params("arbitrary", "arbitrary", "arbitrary"),
        name="retention",
    )(log_g, proj, proj, proj, proj, cos_t, sin_t, *cast_args)


def _att_bias_body(tab_ref, o_ref):
    qb, width = o_ref.shape
    n = pl.cdiv(width + qb - 1, LANES) * LANES
    tab = jnp.broadcast_to(tab_ref[0], (SUBLANES, N_REL))
    u = lax.broadcasted_iota(jnp.int32, (N_REL, n), 1)
    kk = lax.broadcasted_iota(jnp.int32, (N_REL, n), 0)
    rel = jnp.clip(LEFT + qb - 1 - u, -(CHUNK - 1), REL_CLIP) + (CHUNK - 1)
    onehot = jnp.where(rel == kk, 1.0, 0.0).astype(BF16)
    e = jnp.zeros((SUBLANES, n), F32)
    rest = tab
    for _ in range(3):
        piece = rest.astype(BF16)
        e += jnp.dot(piece, onehot, preferred_element_type=F32)
        rest = rest - piece.astype(F32)
    eb = jnp.broadcast_to(e[0:1], (qb, n))
    bias = pltpu.roll(eb, n - (qb - 1), axis=1, stride=1, stride_axis=0)[:, :width]
    r = lax.broadcasted_iota(jnp.int32, (qb, width), 0) // CHUNK
    mc = lax.broadcasted_iota(jnp.int32, (qb, width), 1) // CHUNK
    blocks_before_start = (width // qb - 1) - pl.program_id(0)
    visible = (mc >= r) & (mc <= r + LEFT_CHUNKS) & (mc * CHUNK >= blocks_before_start * qb)
    o_ref[...] = jnp.where(visible, bias * (HEAD_DIM ** 0.5), MASKED)


def attention_bias(rel_table, qb):
    h = rel_table.shape[0]
    width = LEFT + qb
    nkb = width // qb
    return pl.pallas_call(
        _att_bias_body,
        grid=(nkb, h),
        in_specs=[pl.BlockSpec((1, 1, N_REL), lambda v, i: (i, 0, 0))],
        out_specs=pl.BlockSpec((None, None, qb, width), lambda v, i: (v, i, 0, 0)),
        out_shape=jax.ShapeDtypeStruct((nkb, h, qb, width), F32),
        compiler_params=_params("parallel", "parallel"),
        name="attention_bias",
    )(rel_table.reshape(h, 1, N_REL))


def _attention_body(heads, nkb, n_casts, q_ref, *refs):
    k_refs, v_refs = refs[:nkb], refs[nkb:2 * nkb]
    bias_ref = refs[2 * nkb]
    cast_in = refs[2 * nkb + 1:2 * nkb + 1 + n_casts]
    o_ref = refs[2 * nkb + 1 + n_casts]
    cast_out = refs[2 * nkb + 2 + n_casts:]
    for src, dst in zip(cast_in, cast_out):
        dst[...] = src[...].astype(dst.dtype)
    qb = q_ref.shape[0]
    exp2_scale = (HEAD_DIM ** -0.5) * LOG2_E

    def lanes(hh):
        return slice(hh * HEAD_DIM, (hh + 1) * HEAD_DIM)

    def logits(hh, _):
        q = q_ref[:, lanes(hh)]
        return [lax.dot_general(q, k_refs[j][:, lanes(hh)], (((1,), (1,)), ((), ())),
                                preferred_element_type=F32)
                + bias_ref[hh, :, j * qb:(j + 1) * qb] for j in range(nkb)]

    def weights(hh, raw):
        mx = jnp.max(functools.reduce(jnp.maximum, raw), axis=-1, keepdims=True)
        es = [jnp.exp2((s - mx) * exp2_scale) for s in raw]
        den = jnp.sum(functools.reduce(jnp.add, es), axis=-1, keepdims=True)
        return [e.astype(BF16) for e in es], den

    def values(hh, es_den):
        es, den = es_den
        acc = functools.reduce(jnp.add, [
            jnp.dot(es[j], v_refs[j][:, lanes(hh)], preferred_element_type=F32)
            for j in range(nkb)])
        o_ref[:, lanes(hh)] = (acc / den).astype(o_ref.dtype)

    _staggered(heads, (logits, weights, values))


def band_attention(proj, bias, batch, t, n_heads, col0, casts=()):
    m = batch * t
    nkb, _, qb, _ = bias.shape
    nq = t // qb
    hd = HEAD_DIM
    heads = _tile(n_heads, ATTENTION_HEADS)
    ng = n_heads // heads
    assert col0 % heads == 0
    cg0 = col0 // heads

    def kv_spec(which, j):
        back = nkb - 1 - j
        return pl.BlockSpec(
            (qb, heads * hd),
            lambda g, b, i: (b * nq + jnp.maximum(i - back, 0), cg0 + which * ng + g))

    in_specs = [pl.BlockSpec((qb, heads * hd), lambda g, b, i: (b * nq + i, cg0 + g))]
    in_specs += [kv_spec(1, j) for j in range(nkb)]
    in_specs += [kv_spec(2, j) for j in range(nkb)]
    in_specs += [pl.BlockSpec((None, heads, qb, LEFT + qb),
                              lambda g, b, i: (jnp.minimum(i, nkb - 1), g, 0, 0))]
    cast_in, cast_out, cast_shapes, cast_args = _cast_streams(
        casts, ng * batch * nq, lambda g, b, i: (g * batch + b) * nq + i)
    return pl.pallas_call(
        functools.partial(_attention_body, heads, nkb, len(casts)),
        grid=(ng, batch, nq),
        in_specs=in_specs + cast_in,
        out_specs=[pl.BlockSpec((qb, heads * hd), lambda g, b, i: (b * nq + i, g))] + cast_out,
        out_shape=[jax.ShapeDtypeStruct((m, n_heads * hd), BF16)] + cast_shapes,
        compiler_params=_params("arbitrary", "arbitrary", "arbitrary"),
        name="band_attention",
    )(*([proj] * (1 + 2 * nkb)), bias, *cast_args)


def _ffn_up_step(tiles_per_seq, nj, a_ref, ss_ref, wg_ref, wv_ref, cwg_ref, cwv_ref, cbg_ref,
                 cbv_ref, o_ref, cur_ref, prev_ref, halo_ref):
    tm = a_ref.shape[0]
    prev = jnp.maximum(pl.program_id(0) - 1, 0)
    jp = prev % nj
    seq_start = ((prev // nj) % tiles_per_seq) == 0

    above = [halo_ref[half, jp] for half in range(2)]
    for half in range(2):
        halo_ref[half, jp] = prev_ref[half, tm:, :]

    taps, biases = [], []
    for half, (w_ref, b_ref) in enumerate(((cwg_ref, cbg_ref), (cwv_ref, cbv_ref))):
        prev_ref[half, 0:SUBLANES, :] = jnp.where(seq_start, 0.0, above[half])
        w = w_ref[...]
        taps.append([w[k:k + 1] for k in range(CONV_WIDTH)])
        biases.append(b_ref[...])

    def conv(half, r0, rows):
        out = biases[half]
        for k in range(CONV_WIDTH):
            lo = SUBLANES + r0 - (CONV_WIDTH - 1 - k)
            out = out + taps[half][k] * prev_ref[half, lo:lo + rows, :]
        return out

    d = a_ref.shape[1]
    n_chunks = max(1, min(tm // FFN_EPILOGUE_ROWS, d // FFN_K_CHUNK))
    rows, kc = tm // n_chunks, d // n_chunks
    acc = [None, None]
    for c in range(n_chunks):
        a = a_ref[:, c * kc:(c + 1) * kc]
        for half, w_ref in enumerate((wg_ref, wv_ref)):
            part = jnp.dot(a, w_ref[c * kc:(c + 1) * kc, :], preferred_element_type=F32)
            acc[half] = part if acc[half] is None else acc[half] + part
        r0 = c * rows
        g = conv(0, r0, rows)
        val = conv(1, r0, rows)
        o_ref[r0:r0 + rows, :] = (_silu(g) * val).astype(o_ref.dtype)
    scale = _row_scale(ss_ref, d)
    cur_ref[0, SUBLANES:, :] = acc[0] * scale
    cur_ref[1, SUBLANES:, :] = acc[1] * scale


def _ffn_up_body(tiles_per_seq, nj, *refs):
    io_refs, (raw_a, raw_b, halo_ref) = refs[:-3], refs[-3:]
    step = pl.program_id(0)
    run = functools.partial(_ffn_up_step, tiles_per_seq, nj, *io_refs)

    @pl.when(step == 0)
    def _():
        raw_b[...] = jnp.zeros_like(raw_b)
        halo_ref[...] = jnp.zeros_like(halo_ref)

    @pl.when(step % 2 == 0)
    def _():
        run(raw_a, raw_b, halo_ref)

    @pl.when(step % 2 == 1)
    def _():
        run(raw_b, raw_a, halo_ref)


def ffn_up(xg, ss, w, conv_w, conv_b, t):
    m, d = xg.shape
    f = w.shape[1] // 2
    tm, tn = _tile(min(m, t), 1024), _tile(f, 2 * LANES)
    ni, nj = m // tm, f // tn
    n_tiles = ni * nj

    def cur(s):
        c = jnp.minimum(s, n_tiles - 1)
        return c // nj, c % nj

    def prev(s):
        p = jnp.maximum(s - 1, 0)
        return p // nj, p % nj

    def wspec(half):
        return pl.BlockSpec((d, tn), lambda s: (0, half * nj + cur(s)[1]))

    def cspec(rows, half):
        return pl.BlockSpec((rows, tn), lambda s: (0, half * nj + prev(s)[1]))

    raw = pltpu.VMEM((2, SUBLANES + tm, tn), F32)
    return pl.pallas_call(
        functools.partial(_ffn_up_body, t // tm, nj),
        grid=(n_tiles + 1,),
        in_specs=[pl.BlockSpec((tm, d), lambda s: (cur(s)[0], 0)),
                  _ss_in_spec(ss, tm, lambda s: cur(s)[0]), wspec(0), wspec(1),
                  cspec(CONV_WIDTH, 0), cspec(CONV_WIDTH, 1), cspec(1, 0), cspec(1, 1)],
        out_specs=pl.BlockSpec((tm, tn), lambda s: prev(s)),
        out_shape=jax.ShapeDtypeStruct((m, f), BF16),
        scratch_shapes=[raw, raw, pltpu.VMEM((2, nj, SUBLANES, tn), F32)],
        compiler_params=_params("arbitrary"),
        name="ffn_up",
    )(xg, ss, w, w, conv_w, conv_w, conv_b.reshape(1, 2 * f), conv_b.reshape(1, 2 * f))


def kernel(x, ln_mix, w_in, rel_bias, w_out, ln_ffn, w_up, conv_w, conv_b, w_down, ln_final):
    batch, t, d = x.shape
    depth = w_in.shape[0]
    m = batch * t
    n_ret = (d // 2) // HEAD_DIM
    n_att = rel_bias.shape[1]

    cos_t, sin_t = rope_tables(t)
    log_g = jnp.log(1.0 - 2.0 ** (-5.0 - jnp.arange(n_ret, dtype=F32)))
    ret_block = _tile(t, 256)
    att_block = _tile(LEFT, 256)

    h = x.reshape(m, d)
    xg, ss = prescale(h, ln_mix[0])
    w_in_b = cast_weight(w_in, 0)
    for layer in range(depth):
        proj, w_up_b = in_proj(xg, ss, w_in_b, casts=[(w_up, layer)])
        ro, w_dn = retention(proj, cos_t, sin_t, log_g, batch, t, n_ret, ret_block,
                             casts=[(w_down, layer)])
        bias = attention_bias(rel_bias[layer], att_block)
        casts = [(w_out, layer)] + ([(w_in, layer + 1)] if layer + 1 < depth else [])
        ao, w_o, *w_next = band_attention(proj, bias, batch, t, n_att, 4 * n_ret, casts=casts)
        w_in_b = w_next[0] if w_next else None
        h, xg, ss = out_proj_residual(ro, ao, w_o, h, ln_ffn[layer])

        act = ffn_up(xg, ss, w_up_b, conv_w[layer], conv_b[layer], t)
        if layer + 1 < depth:
            h, xg, ss = down_proj_residual(act, w_dn, h, ln_mix[layer + 1])
        else:
            h = down_proj_residual(act, w_dn, h)
    return rmsnorm(h, ln_final, F32).reshape(batch, t, d)
```

```python
import functools

import jax
import jax.numpy as jnp
from jax import lax
from jax.experimental import pallas as pl
from jax.experimental.pallas import tpu as pltpu

CHUNK = 64
HEAD_DIM = 128
LEFT_CHUNKS = 8
LEFT = LEFT_CHUNKS * CHUNK
REL_CLIP = 128
N_REL = REL_CLIP + CHUNK
CONV_WIDTH = 3
ROPE_BASE = 10000.0
EPS = 1e-6

LANES = 128
SUBLANES = 8
VMEM_LIMIT = 56 * 1024 * 1024
RETENTION_HEADS = 8
ATTENTION_HEADS = 16
FFN_EPILOGUE_ROWS = 64
FFN_K_CHUNK = 256
MASKED = -1e30
LOG2_E = 1.4426950408889634

F32 = jnp.float32
BF16 = jnp.bfloat16


def _params(*sem):
    return pltpu.CompilerParams(dimension_semantics=sem, vmem_limit_bytes=VMEM_LIMIT)


def _tile(n, want):
    t = min(n, want)
    while n % t:
        t //= 2
    return t


def _rmsnorm_body(x_ref, g_ref, o_ref):
    x = x_ref[...]
    ms = jnp.mean(x * x, axis=-1, keepdims=True)
    o_ref[...] = (x * lax.rsqrt(ms + EPS) * g_ref[...]).astype(o_ref.dtype)


def rmsnorm(x, gain, out_dtype):
    m, d = x.shape
    tm = _tile(m, 256)
    return pl.pallas_call(
        _rmsnorm_body,
        grid=(m // tm,),
        in_specs=[pl.BlockSpec((tm, d), lambda i: (i, 0)),
                  pl.BlockSpec((1, d), lambda i: (0, 0))],
        out_specs=pl.BlockSpec((tm, d), lambda i: (i, 0)),
        out_shape=jax.ShapeDtypeStruct((m, d), out_dtype),
        compiler_params=_params("parallel"),
        name="rmsnorm",
    )(x, gain.reshape(1, d))


def _sum_squares(x):
    return jnp.sum(x * x, axis=-1, keepdims=True)


def _row_scale(ss_ref, d):
    return lax.rsqrt(jnp.sum(ss_ref[...], axis=0) / d + EPS)


def _ss_in_spec(ss, tm, row_of):
    return pl.BlockSpec((ss.shape[0], tm, 1), lambda *idx: (0, row_of(*idx), 0))


def _prescale_body(x_ref, g_ref, o_ref, ss_ref):
    x = x_ref[...]
    o_ref[...] = (x * g_ref[...]).astype(o_ref.dtype)
    ss_ref[...] = _sum_squares(x)


def prescale(x, gain):
    m, d = x.shape
    tm = _tile(m, 256)
    return pl.pallas_call(
        _prescale_body,
        grid=(m // tm,),
        in_specs=[pl.BlockSpec((tm, d), lambda i: (i, 0)),
                  pl.BlockSpec((1, d), lambda i: (0, 0))],
        out_specs=[pl.BlockSpec((tm, d), lambda i: (i, 0)),
                   pl.BlockSpec((None, tm, 1), lambda i: (0, i, 0))],
        out_shape=[jax.ShapeDtypeStruct((m, d), BF16),
                   jax.ShapeDtypeStruct((1, m, 1), F32)],
        compiler_params=_params("parallel"),
        name="prescale",
    )(x, gain.reshape(1, d))


def _in_proj_body(n_casts, a_ref, ss_ref, b_ref, *refs):
    cast_in, o_ref, cast_out = refs[:n_casts], refs[n_casts], refs[n_casts + 1:]
    for src, dst in zip(cast_in, cast_out):
        dst[...] = src[...].astype(dst.dtype)
    acc = jnp.dot(a_ref[...], b_ref[...], preferred_element_type=F32)
    o_ref[...] = (acc * _row_scale(ss_ref, a_ref.shape[1])).astype(o_ref.dtype)


def in_proj(a, ss, b, casts=()):
    m, k = a.shape
    n = b.shape[1]
    tm, tn = _tile(m, 1024), _tile(n, 1024)
    nj = n // tn
    cast_in, cast_out, cast_shapes, cast_args = _cast_streams(
        casts, (m // tm) * nj, lambda i, j: i * nj + j)
    return pl.pallas_call(
        functools.partial(_in_proj_body, len(casts)),
        grid=(m // tm, nj),
        in_specs=[pl.BlockSpec((tm, k), lambda i, j: (i, 0)),
                  _ss_in_spec(ss, tm, lambda i, j: i),
                  pl.BlockSpec((k, tn), lambda i, j: (0, j))] + cast_in,
        out_specs=[pl.BlockSpec((tm, tn), lambda i, j: (i, j))] + cast_out,
        out_shape=[jax.ShapeDtypeStruct((m, n), BF16)] + cast_shapes,
        compiler_params=_params("arbitrary", "arbitrary"),
        name="in_proj",
    )(a, ss, b, *cast_args)


def _emit_residual(h, col_tile, g_ref, o_ref, hg_ref, ss_ref):
    o_ref[...] = h
    hg_ref[...] = (h * g_ref[...]).astype(hg_ref.dtype)
    part = _sum_squares(h)

    @pl.when(col_tile == 0)
    def _():
        ss_ref[...] = part

    @pl.when(col_tile > 0)
    def _():
        ss_ref[...] += part


def _residual_out(m, n, tm, tn, idx):
    specs = [pl.BlockSpec((tm, tn), lambda *g: idx(*g)),
             pl.BlockSpec((tm, tn), lambda *g: idx(*g)),
             pl.BlockSpec((None, tm, 1), lambda *g: (0, idx(*g)[0], 0))]
    shapes = [jax.ShapeDtypeStruct((m, n), F32), jax.ShapeDtypeStruct((m, n), BF16),
              jax.ShapeDtypeStruct((1, m, 1), F32)]
    return specs, shapes


def _out_proj_body(a1_ref, a2_ref, b1_ref, b2_ref, r_ref, g_ref, o_ref, hg_ref, ss_ref):
    acc = jnp.dot(a1_ref[...], b1_ref[...], preferred_element_type=F32)
    acc += jnp.dot(a2_ref[...], b2_ref[...], preferred_element_type=F32)
    _emit_residual(r_ref[...] + acc, pl.program_id(1), g_ref, o_ref, hg_ref, ss_ref)


def out_proj_residual(a1, a2, w, res, next_gain):
    m, k1 = a1.shape
    k2 = a2.shape[1]
    n = w.shape[1]
    assert k1 == k2 and w.shape[0] == k1 + k2
    tm, tn = _tile(m, 1024), _tile(n, 512)
    out_specs, out_shape = _residual_out(m, n, tm, tn, lambda i, j: (i, j))
    return pl.pallas_call(
        _out_proj_body,
        grid=(m // tm, n // tn),
        in_specs=[pl.BlockSpec((tm, k1), lambda i, j: (i, 0)),
                  pl.BlockSpec((tm, k2), lambda i, j: (i, 0)),
                  pl.BlockSpec((k1, tn), lambda i, j: (0, j)),
                  pl.BlockSpec((k2, tn), lambda i, j: (1, j)),
                  pl.BlockSpec((tm, tn), lambda i, j: (i, j)),
                  pl.BlockSpec((1, tn), lambda i, j: (0, j))],
        out_specs=out_specs,
        out_shape=out_shape,
        compiler_params=_params("parallel", "arbitrary"),
        name="out_proj",
    )(a1, a2, w, w, res, next_gain.reshape(1, n))


def _down_proj_body(nk, k_last, emit_next, a_ref, b_ref, r_ref, *refs):
    col_tile = pl.program_id(1)
    k = pl.program_id(2)
    acc_ref = refs[-1]

    @pl.when(k == 0)
    def _():
        acc_ref[...] = jnp.dot(a_ref[...], b_ref[...], preferred_element_type=F32)

    @pl.when((k > 0) & (k < nk - 1))
    def _():
        acc_ref[...] += jnp.dot(a_ref[...], b_ref[...], preferred_element_type=F32)

    @pl.when(k == nk - 1)
    def _():
        tail = jnp.dot(a_ref[:, :k_last], b_ref[:k_last, :], preferred_element_type=F32)
        h = r_ref[...] + (acc_ref[...] + tail)
        if emit_next:
            _emit_residual(h, col_tile, *refs[:4])
        else:
            refs[0][...] = h


def down_proj_residual(a, w, res, next_gain=None):
    m, k = a.shape
    n = w.shape[1]
    tm, tn = _tile(m, 1024), _tile(n, 1024)
    tk = -(-k // (4 * 2 * LANES)) * 2 * LANES
    nk = -(-k // tk)
    assert nk >= 2
    k_last = k - (nk - 1) * tk
    emit_next = next_gain is not None
    in_specs = [pl.BlockSpec((tm, tk), lambda i, j, kk: (i, kk)),
                pl.BlockSpec((tk, tn), lambda i, j, kk: (kk, j)),
                pl.BlockSpec((tm, tn), lambda i, j, kk: (i, j))]
    args = [a, w, res]
    out_specs, out_shape = _residual_out(m, n, tm, tn, lambda i, j, kk: (i, j))
    if emit_next:
        in_specs.append(pl.BlockSpec((1, tn), lambda i, j, kk: (0, j)))
        args.append(next_gain.reshape(1, n))
    else:
        out_specs, out_shape = out_specs[0], out_shape[0]
    return pl.pallas_call(
        functools.partial(_down_proj_body, nk, k_last, emit_next),
        grid=(m // tm, n // tn, nk),
        in_specs=in_specs,
        out_specs=out_specs,
        out_shape=out_shape,
        scratch_shapes=[pltpu.VMEM((tm, tn), F32)],
        compiler_params=_params("parallel", "arbitrary", "arbitrary"),
        name="down_proj",
    )(*args)


def _cast_body(x_ref, o_ref):
    o_ref[...] = x_ref[...].astype(o_ref.dtype)


def cast_weight(w, layer):
    _, k, n = w.shape
    tk, tn = _tile(k, 1024), _tile(n, 2048)
    return pl.pallas_call(
        _cast_body,
        grid=(k // tk, n // tn),
        in_specs=[pl.BlockSpec((None, tk, tn), lambda i, j: (layer, i, j))],
        out_specs=pl.BlockSpec((tk, tn), lambda i, j: (i, j)),
        out_shape=jax.ShapeDtypeStruct((k, n), BF16),
        compiler_params=_params("parallel", "parallel"),
        name="cast_weight",
    )(w)


def _cast_stream(w, layer, n_steps, step_of):
    _, k, n = w.shape
    rows = next(r for r in (2 * SUBLANES << p for p in range(16))
                if k % r == 0 and k // r <= n_steps)
    last = k // rows - 1

    def block(*g):
        return jnp.minimum(step_of(*g), last)

    return (pl.BlockSpec((None, rows, n), lambda *g: (layer, block(*g), 0)),
            pl.BlockSpec((rows, n), lambda *g: (block(*g), 0)),
            jax.ShapeDtypeStruct((k, n), BF16))


def _cast_streams(casts, n_steps, step_of):
    specs = [_cast_stream(w, layer, n_steps, step_of) for w, layer in casts]
    return ([s[0] for s in specs], [s[1] for s in specs], [s[2] for s in specs],
            [w for w, _ in casts])


def _rope_body(inv_ref, cos_ref, sin_ref):
    tt = cos_ref.shape[0]
    row = lax.broadcasted_iota(jnp.int32, (tt, LANES), 0) + pl.program_id(0) * tt
    lane = lax.broadcasted_iota(jnp.int32, (tt, LANES), 1)
    ang = row.astype(F32) * inv_ref[...]
    cos_ref[...] = jnp.cos(ang)
    sin_ref[...] = jnp.where(lane < HEAD_DIM // 2, -1.0, 1.0) * jnp.sin(ang)


def rope_tables(t):
    half = HEAD_DIM // 2
    inv = 1.0 / (ROPE_BASE ** jnp.linspace(0.0, 1.0, half, dtype=F32))
    inv2 = jnp.concatenate([inv, inv]).reshape(1, HEAD_DIM)
    tt = _tile(t, 512)
    return pl.pallas_call(
        _rope_body,
        grid=(t // tt,),
        in_specs=[pl.BlockSpec((1, HEAD_DIM), lambda i: (0, 0))],
        out_specs=[pl.BlockSpec((tt, HEAD_DIM), lambda i: (i, 0)),
                   pl.BlockSpec((tt, HEAD_DIM), lambda i: (i, 0))],
        out_shape=[jax.ShapeDtypeStruct((t, HEAD_DIM), F32)] * 2,
        compiler_params=_params("parallel"),
        name="rope_tables",
    )(inv2)


def _silu(x):
    return x * jax.nn.sigmoid(x)


def _staggered(n_items, stages):
    carried = [None] * n_items
    for t in range(n_items + len(stages) - 1):
        for s, stage in enumerate(stages):
            item = t - s
            if 0 <= item < n_items:
                carried[item] = stage(item, carried[item])


def _retention_body(heads, n_casts, logg_ref, q_ref, k_ref, v_ref, g_ref, cos_ref, sin_ref,
                    *refs):
    cast_in, o_ref, cast_out = refs[:n_casts], refs[n_casts], refs[n_casts + 1:2 * n_casts + 1]
    state_ref, intra_ref, qdec_ref, kdec_ref = refs[2 * n_casts + 1:]
    group = pl.program_id(1)
    step = pl.program_id(2)
    c = q_ref.shape[0]

    for src, dst in zip(cast_in, cast_out):
        dst[...] = src[...].astype(dst.dtype)

    @pl.when(step == 0)
    def _():
        state_ref[...] = jnp.zeros_like(state_ref)
        row = lax.broadcasted_iota(jnp.int32, (c, c), 0)
        col = lax.broadcasted_iota(jnp.int32, (c, c), 1)
        diff = (row - col).astype(F32)
        pos = lax.broadcasted_iota(jnp.int32, (c, HEAD_DIM), 0).astype(F32)
        for hh in range(heads):
            lg = logg_ref[group * heads + hh]
            intra_ref[hh] = jnp.where(diff >= 0, jnp.exp(jnp.maximum(diff, 0.0) * lg), 0.0)
            qdec_ref[hh] = jnp.exp((pos + 1.0) * lg)
            kdec_ref[hh] = jnp.exp((c - 1.0 - pos) * lg)

    cos = cos_ref[...]
    sin = sin_ref[...]

    def rot(x):
        return x * cos + pltpu.roll(x, HEAD_DIM // 2, axis=1) * sin

    def lanes(hh):
        return slice(hh * HEAD_DIM, (hh + 1) * HEAD_DIM)

    def rotate(hh, _):
        q = rot(q_ref[:, lanes(hh)].astype(F32))
        k = rot(k_ref[:, lanes(hh)].astype(F32)) * (HEAD_DIM ** -0.5)
        return q.astype(BF16), k.astype(BF16), (k * kdec_ref[hh]).astype(BF16)

    def scores(hh, qk):
        qb, kb, kd = qk
        s = lax.dot_general(qb, kb, (((1,), (1,)), ((), ())), preferred_element_type=F32)
        return qb, kd, (s * intra_ref[hh]).astype(BF16)

    def outputs(hh, qks):
        qb, kd, s = qks
        v = v_ref[:, lanes(hh)]
        state = state_ref[hh]
        o = jnp.dot(s, v, preferred_element_type=F32)
        o += jnp.dot(qb, state.astype(BF16), preferred_element_type=F32) * qdec_ref[hh]
        kv = lax.dot_general(kd, v, (((0,), (0,)), ((), ())), preferred_element_type=F32)
        block_decay = jnp.exp(jnp.full((1, HEAD_DIM), float(c), F32)
                              * logg_ref[group * heads + hh])
        state_ref[hh] = state * block_decay + kv
        return o

    def norm_gate(hh, o):
        o = o * lax.rsqrt(jnp.mean(o * o, axis=-1, keepdims=True) + EPS)
        o_ref[:, lanes(hh)] = (_silu(g_ref[:, lanes(hh)].astype(F32)) * o).astype(o_ref.dtype)

    _staggered(heads, (rotate, scores, outputs, norm_gate))


def retention(proj, cos_t, sin_t, log_g, batch, t, n_heads, block, casts=()):
    m = batch * t
    nt = t // block
    hd = HEAD_DIM
    heads = _tile(n_heads, RETENTION_HEADS)
    ng = n_heads // heads
    cast_in, cast_out, cast_shapes, cast_args = _cast_streams(
        casts, batch * ng * nt, lambda b, g, i, lg: (b * ng + g) * nt + i)

    def col(which):
        return pl.BlockSpec((block, heads * hd),
                            lambda b, g, i, lg: (b * nt + i, which * ng + g))

    tab = pl.BlockSpec((block, hd), lambda b, g, i, lg: (i, 0))
    return pl.pallas_call(
        functools.partial(_retention_body, heads, len(casts)),
        grid_spec=pltpu.PrefetchScalarGridSpec(
            num_scalar_prefetch=1,
            grid=(batch, ng, nt),
            in_specs=[col(0), col(1), col(2), col(3), tab, tab] + cast_in,
            out_specs=[pl.BlockSpec((block, heads * hd),
                                    lambda b, g, i, lg: (b * nt + i, g))] + cast_out,
            scratch_shapes=[pltpu.VMEM((heads, hd, hd), F32),
                            pltpu.VMEM((heads, block, block), F32),
                            pltpu.VMEM((heads, block, hd), F32),
                            pltpu.VMEM((heads, block, hd), F32)],
        ),
        out_shape=[jax.ShapeDtypeStruct((m, n_heads * hd), BF16)] + cast_shapes,
        compiler_params=_params("arbitrary", "arbitrary", "arbitrary"),
        name="retention",
    )(log_g, proj, proj, proj, proj, cos_t, sin_t, *cast_args)


def _att_bias_body(tab_ref, o_ref):
    qb, width = o_ref.shape
    n = pl.cdiv(width + qb - 1, LANES) * LANES
    tab = jnp.broadcast_to(tab_ref[0], (SUBLANES, N_REL))
    u = lax.broadcasted_iota(jnp.int32, (N_REL, n), 1)
    kk = lax.broadcasted_iota(jnp.int32, (N_REL, n), 0)
    rel = jnp.clip(LEFT + qb - 1 - u, -(CHUNK - 1), REL_CLIP) + (CHUNK - 1)
    onehot = jnp.where(rel == kk, 1.0, 0.0).astype(BF16)
    e = jnp.zeros((SUBLANES, n), F32)
    rest = tab
    for _ in range(3):
        piece = rest.astype(BF16)
        e += jnp.dot(piece, onehot, preferred_element_type=F32)
        rest = rest - piece.astype(F32)
    eb = jnp.broadcast_to(e[0:1], (qb, n))
    bias = pltpu.roll(eb, n - (qb - 1), axis=1, stride=1, stride_axis=0)[:, :width]
    r = lax.broadcasted_iota(jnp.int32, (qb, width), 0) // CHUNK
    mc = lax.broadcasted_iota(jnp.int32, (qb, width), 1) // CHUNK
    blocks_before_start = (width // qb - 1) - pl.program_id(0)
    visible = (mc >= r) & (mc <= r + LEFT_CHUNKS) & (mc * CHUNK >= blocks_before_start * qb)
    o_ref[...] = jnp.where(visible, bias * (HEAD_DIM ** 0.5), MASKED)


def attention_bias(rel_table, qb):
    h = rel_table.shape[0]
    width = LEFT + qb
    nkb = width // qb
    return pl.pallas_call(
        _att_bias_body,
        grid=(nkb, h),
        in_specs=[pl.BlockSpec((1, 1, N_REL), lambda v, i: (i, 0, 0))],
        out_specs=pl.BlockSpec((None, None, qb, width), lambda v, i: (v, i, 0, 0)),
        out_shape=jax.ShapeDtypeStruct((nkb, h, qb, width), F32),
        compiler_params=_params("parallel", "parallel"),
        name="attention_bias",
    )(rel_table.reshape(h, 1, N_REL))


def _attention_body(heads, nkb, n_casts, q_ref, *refs):
    k_refs, v_refs = refs[:nkb], refs[nkb:2 * nkb]
    bias_ref = refs[2 * nkb]
    cast_in = refs[2 * nkb + 1:2 * nkb + 1 + n_casts]
    o_ref = refs[2 * nkb + 1 + n_casts]
    cast_out = refs[2 * nkb + 2 + n_casts:]
    for src, dst in zip(cast_in, cast_out):
        dst[...] = src[...].astype(dst.dtype)
    qb = q_ref.shape[0]
    exp2_scale = (HEAD_DIM ** -0.5) * LOG2_E

    def lanes(hh):
        return slice(hh * HEAD_DIM, (hh + 1) * HEAD_DIM)

    def logits(hh, _):
        q = q_ref[:, lanes(hh)]
        return [lax.dot_general(q, k_refs[j][:, lanes(hh)], (((1,), (1,)), ((), ())),
                                preferred_element_type=F32)
                + bias_ref[hh, :, j * qb:(j + 1) * qb] for j in range(nkb)]

    def weights(hh, raw):
        mx = jnp.max(functools.reduce(jnp.maximum, raw), axis=-1, keepdims=True)
        es = [jnp.exp2((s - mx) * exp2_scale) for s in raw]
        den = jnp.sum(functools.reduce(jnp.add, es), axis=-1, keepdims=True)
        return [e.astype(BF16) for e in es], den

    def values(hh, es_den):
        es, den = es_den
        acc = functools.reduce(jnp.add, [
            jnp.dot(es[j], v_refs[j][:, lanes(hh)], preferred_element_type=F32)
            for j in range(nkb)])
        o_ref[:, lanes(hh)] = (acc / den).astype(o_ref.dtype)

    _staggered(heads, (logits, weights, values))


def band_attention(proj, bias, batch, t, n_heads, col0, casts=()):
    m = batch * t
    nkb, _, qb, _ = bias.shape
    nq = t // qb
    hd = HEAD_DIM
    heads = _tile(n_heads, ATTENTION_HEADS)
    ng = n_heads // heads
    assert col0 % heads == 0
    cg0 = col0 // heads

    def kv_spec(which, j):
        back = nkb - 1 - j
        return pl.BlockSpec(
            (qb, heads * hd),
            lambda g, b, i: (b * nq + jnp.maximum(i - back, 0), cg0 + which * ng + g))

    in_specs = [pl.BlockSpec((qb, heads * hd), lambda g, b, i: (b * nq + i, cg0 + g))]
    in_specs += [kv_spec(1, j) for j in range(nkb)]
    in_specs += [kv_spec(2, j) for j in range(nkb)]
    in_specs += [pl.BlockSpec((None, heads, qb, LEFT + qb),
                              lambda g, b, i: (jnp.minimum(i, nkb - 1), g, 0, 0))]
    cast_in, cast_out, cast_shapes, cast_args = _cast_streams(
        casts, ng * batch * nq, lambda g, b, i: (g * batch + b) * nq + i)
    return pl.pallas_call(
        functools.partial(_attention_body, heads, nkb, len(casts)),
        grid=(ng, batch, nq),
        in_specs=in_specs + cast_in,
        out_specs=[pl.BlockSpec((qb, heads * hd), lambda g, b, i: (b * nq + i, g))] + cast_out,
        out_shape=[jax.ShapeDtypeStruct((m, n_heads * hd), BF16)] + cast_shapes,
        compiler_params=_params("arbitrary", "arbitrary", "arbitrary"),
        name="band_attention",
    )(*([proj] * (1 + 2 * nkb)), bias, *cast_args)


def _ffn_up_step(tiles_per_seq, nj, a_ref, ss_ref, wg_ref, wv_ref, cwg_ref, cwv_ref, cbg_ref,
                 cbv_ref, o_ref, cur_ref, prev_ref, halo_ref):
    tm = a_ref.shape[0]
    prev = jnp.maximum(pl.program_id(0) - 1, 0)
    jp = prev % nj
    seq_start = ((prev // nj) % tiles_per_seq) == 0

    above = [halo_ref[half, jp] for half in range(2)]
    for half in range(2):
        halo_ref[half, jp] = prev_ref[half, tm:, :]

    taps, biases = [], []
    for half, (w_ref, b_ref) in enumerate(((cwg_ref, cbg_ref), (cwv_ref, cbv_ref))):
        prev_ref[half, 0:SUBLANES, :] = jnp.where(seq_start, 0.0, above[half])
        w = w_ref[...]
        taps.append([w[k:k + 1] for k in range(CONV_WIDTH)])
        biases.append(b_ref[...])

    def conv(half, r0, rows):
        out = biases[half]
        for k in range(CONV_WIDTH):
            lo = SUBLANES + r0 - (CONV_WIDTH - 1 - k)
            out = out + taps[half][k] * prev_ref[half, lo:lo + rows, :]
        return out

    d = a_ref.shape[1]
    n_chunks = max(1, min(tm // FFN_EPILOGUE_ROWS, d // FFN_K_CHUNK))
    rows, kc = tm // n_chunks, d // n_chunks
    acc = [None, None]
    for c in range(n_chunks):
        a = a_ref[:, c * kc:(c + 1) * kc]
        for half, w_ref in enumerate((wg_ref, wv_ref)):
            part = jnp.dot(a, w_ref[c * kc:(c + 1) * kc, :], preferred_element_type=F32)
            acc[half] = part if acc[half] is None else acc[half] + part
        r0 = c * rows
        g = conv(0, r0, rows)
        val = conv(1, r0, rows)
        o_ref[r0:r0 + rows, :] = (_silu(g) * val).astype(o_ref.dtype)
    scale = _row_scale(ss_ref, d)
    cur_ref[0, SUBLANES:, :] = acc[0] * scale
    cur_ref[1, SUBLANES:, :] = acc[1] * scale


def _ffn_up_body(tiles_per_seq, nj, *refs):
    io_refs, (raw_a, raw_b, halo_ref) = refs[:-3], refs[-3:]
    step = pl.program_id(0)
    run = functools.partial(_ffn_up_step, tiles_per_seq, nj, *io_refs)

    @pl.when(step == 0)
    def _():
        raw_b[...] = jnp.zeros_like(raw_b)
        halo_ref[...] = jnp.zeros_like(halo_ref)

    @pl.when(step % 2 == 0)
    def _():
        run(raw_a, raw_b, halo_ref)

    @pl.when(step % 2 == 1)
    def _():
        run(raw_b, raw_a, halo_ref)


def ffn_up(xg, ss, w, conv_w, conv_b, t):
    m, d = xg.shape
    f = w.shape[1] // 2
    tm, tn = _tile(min(m, t), 1024), _tile(f, 2 * LANES)
    ni, nj = m // tm, f // tn
    n_tiles = ni * nj

    def cur(s):
        c = jnp.minimum(s, n_tiles - 1)
        return c // nj, c % nj

    def prev(s):
        p = jnp.maximum(s - 1, 0)
        return p // nj, p % nj

    def wspec(half):
        return pl.BlockSpec((d, tn), lambda s: (0, half * nj + cur(s)[1]))

    def cspec(rows, half):
        return pl.BlockSpec((rows, tn), lambda s: (0, half * nj + prev(s)[1]))

    raw = pltpu.VMEM((2, SUBLANES + tm, tn), F32)
    return pl.pallas_call(
        functools.partial(_ffn_up_body, t // tm, nj),
        grid=(n_tiles + 1,),
        in_specs=[pl.BlockSpec((tm, d), lambda s: (cur(s)[0], 0)),
                  _ss_in_spec(ss, tm, lambda s: cur(s)[0]), wspec(0), wspec(1),
                  cspec(CONV_WIDTH, 0), cspec(CONV_WIDTH, 1), cspec(1, 0), cspec(1, 1)],
        out_specs=pl.BlockSpec((tm, tn), lambda s: prev(s)),
        out_shape=jax.ShapeDtypeStruct((m, f), BF16),
        scratch_shapes=[raw, raw, pltpu.VMEM((2, nj, SUBLANES, tn), F32)],
        compiler_params=_params("arbitrary"),
        name="ffn_up",
    )(xg, ss, w, w, conv_w, conv_w, conv_b.reshape(1, 2 * f), conv_b.reshape(1, 2 * f))


def kernel(x, ln_mix, w_in, rel_bias, w_out, ln_ffn, w_up, conv_w, conv_b, w_down, ln_final):
    batch, t, d = x.shape
    depth = w_in.shape[0]
    m = batch * t
    n_ret = (d // 2) // HEAD_DIM
    n_att = rel_bias.shape[1]

    cos_t, sin_t = rope_tables(t)
    log_g = jnp.log(1.0 - 2.0 ** (-5.0 - jnp.arange(n_ret, dtype=F32)))
    ret_block = _tile(t, 256)
    att_block = _tile(LEFT, 256)

    h = x.reshape(m, d)
    xg, ss = prescale(h, ln_mix[0])
    w_in_b = cast_weight(w_in, 0)
    for layer in range(depth):
        proj, w_up_b = in_proj(xg, ss, w_in_b, casts=[(w_up, layer)])
        casts = [(w_down, layer)] + ([(w_in, layer + 1)] if layer + 1 < depth else [])
        ro, w_dn, *w_next = retention(proj, cos_t, sin_t, log_g, batch, t, n_ret, ret_block,
                                      casts=casts)
        w_in_b = w_next[0] if w_next else None
        bias = attention_bias(rel_bias[layer], att_block)
        ao, w_o = band_attention(proj, bias, batch, t, n_att, 4 * n_ret,
                                 casts=[(w_out, layer)])
        h, xg, ss = out_proj_residual(ro, ao, w_o, h, ln_ffn[layer])

        act = ffn_up(xg, ss, w_up_b, conv_w[layer], conv_b[layer], t)
        if layer + 1 < depth:
            h, xg, ss = down_proj_residual(act, w_dn, h, ln_mix[layer + 1])
        else:
            h = down_proj_residual(act, w_dn, h)
    return rmsnorm(h, ln_final, F32).reshape(batch, t, d)
```
